```python
import math
import jax, jax.numpy as jnp
from jax import lax
import numpy as np

D_MODEL = 4096
BATCH = 2
SEQ = 8192
DEPTH = 2

LRU_WIDTH = D_MODEL // 2
LRU_BLOCKS = 16
LRU_BLOCK = LRU_WIDTH // LRU_BLOCKS
CONV_WIDTH = 4
LRU_C = 8.0
ATTN_HEADS = 16
HEAD_DIM = 64
V_DIM = 2 * HEAD_DIM
ATTN_WIDTH = ATTN_HEADS * V_DIM
Q_BLOCK = 128
ROPE_THETA = 10000.0
IN_SPLITS = (LRU_WIDTH, LRU_WIDTH, ATTN_WIDTH, ATTN_WIDTH, ATTN_WIDTH, D_MODEL, D_MODEL)
IN_WIDTH = 2 * LRU_WIDTH + 3 * ATTN_WIDTH + 2 * D_MODEL
PEER_KEYS = 128
PEER_EXPERTS = PEER_KEYS * PEER_KEYS
PEER_HEADS = 8
PEER_TOPK = 16
PEER_QDIM = 256
PEER_HALF = PEER_QDIM // 2
PEER_CHUNK = 128
N_MOD = 6
EPS = 1e-6
NEG_INF = -1e30

kernel_name = 'hybrid_rglru_diffattn_peer_block'


def rms_norm(x, g):
    xf = x.astype(jnp.float32)
    y = xf * lax.rsqrt(jnp.mean(xf * xf, axis=-1, keepdims=True) + EPS)
    return (y * g.astype(jnp.float32)).astype(x.dtype)


def partition(z):
    outs = []
    start = 0
    for size in IN_SPLITS:
        outs.append(z[..., start:start + size])
        start += size
    return outs


def rope(t, cos, sin):
    half = t.shape[-1] // 2
    t1, t2 = t[..., :half], t[..., half:]
    return jnp.concatenate([t1 * cos - t2 * sin, t2 * cos + t1 * sin], axis=-1)


def rg_lru_branch(xb, conv_w, conv_b, w_a, b_a, w_x, b_x, lru_lambda):
    B, S, W = xb.shape
    xc = lax.conv_general_dilated(
        xb, conv_w[:, None, :].astype(xb.dtype), window_strides=(1,),
        padding=[(CONV_WIDTH - 1, 0)], dimension_numbers=('NWC', 'WIO', 'NWC'),
        feature_group_count=W) + conv_b
    xh = xc.reshape(B, S, LRU_BLOCKS, LRU_BLOCK)
    r = jax.nn.sigmoid(jnp.einsum('bshi,hij->bshj', xh, w_a).reshape(B, S, W) + b_a)
    i = jax.nn.sigmoid(jnp.einsum('bshi,hij->bshj', xh, w_x).reshape(B, S, W) + b_x)
    log_a = -LRU_C * r.astype(jnp.float32) * jax.nn.softplus(-lru_lambda.astype(jnp.float32))
    a = jnp.exp(log_a)
    u = jnp.sqrt(-jnp.expm1(2.0 * log_a)) * (i * xc).astype(jnp.float32)

    def combine(e1, e2):
        a1, b1 = e1
        a2, b2 = e2
        return a1 * a2, a2 * b1 + b2

    _, h = lax.associative_scan(combine, (a, u), axis=1)
    return h.astype(xb.dtype)


def diff_attention(q, k, v, positions, q_norm_g, k_norm_g, lq1, lk1, lq2, lk2, subln_g, lam_init):
    B, S, _ = q.shape
    q = rms_norm(q.reshape(B, S, ATTN_HEADS, 2, HEAD_DIM), q_norm_g)
    k = rms_norm(k.reshape(B, S, ATTN_HEADS, 2, HEAD_DIM), k_norm_g)
    v = v.reshape(B, S, ATTN_HEADS, V_DIM)
    inv_freq = ROPE_THETA ** (-jnp.arange(0, HEAD_DIM, 2, dtype=jnp.float32) / HEAD_DIM)
    ang = positions.astype(jnp.float32)[..., None] * inv_freq
    cos = jnp.cos(ang)[:, :, None, None, :].astype(q.dtype)
    sin = jnp.sin(ang)[:, :, None, None, :].astype(q.dtype)
    q = rope(q, cos, sin)
    k = rope(k, cos, sin)
    lam = (jnp.exp(jnp.sum(lq1.astype(jnp.float32) * lk1.astype(jnp.float32)))
           - jnp.exp(jnp.sum(lq2.astype(jnp.float32) * lk2.astype(jnp.float32))) + lam_init)
    scale = HEAD_DIM ** -0.5
    nb = S // Q_BLOCK
    qt = q.transpose(0, 2, 3, 1, 4)
    kt = k.transpose(0, 2, 3, 1, 4)
    vt = v.transpose(0, 2, 1, 3)
    qb = qt.reshape(B, ATTN_HEADS, 2, nb, Q_BLOCK, HEAD_DIM).transpose(3, 0, 1, 2, 4, 5)
    k_idx = jnp.arange(S)

    def block(args):
        q_blk, bi = args
        q_idx = bi * Q_BLOCK + jnp.arange(Q_BLOCK)
        s = jnp.einsum('bhnqd,bhnkd->bhnqk', q_blk, kt,
                       preferred_element_type=jnp.float32) * scale
        mask = k_idx[None, :] <= q_idx[:, None]
        s = jnp.where(mask, s, NEG_INF)
        p = jax.nn.softmax(s, axis=-1)
        w = p[:, :, 0] - lam * p[:, :, 1]
        return jnp.einsum('bhqk,bhkd->bhqd', w.astype(vt.dtype), vt)

    o = lax.map(block, (qb, jnp.arange(nb)))
    o = o.transpose(1, 0, 3, 2, 4).reshape(B, S, ATTN_HEADS, V_DIM)
    o = rms_norm(o, subln_g) * (1.0 - lam_init)
    return o.reshape(B, S, ATTN_WIDTH)


def peer_ffn(h, w_pq, sub_keys, u_tab, v_tab):
    B, S, D = h.shape
    T = B * S
    ht = h.reshape(T, D)
    q = (ht @ w_pq).reshape(T, PEER_HEADS, 2, PEER_HALF)
    s = jnp.einsum('thnd,nkd->thnk', q, sub_keys, preferred_element_type=jnp.float32)
    v1, i1 = lax.top_k(s[:, :, 0], PEER_TOPK)
    v2, i2 = lax.top_k(s[:, :, 1], PEER_TOPK)
    cand = (v1[..., :, None] + v2[..., None, :]).reshape(T, PEER_HEADS, PEER_TOPK * PEER_TOPK)
    best, pos = lax.top_k(cand, PEER_TOPK)
    e = (jnp.take_along_axis(i1, pos // PEER_TOPK, axis=-1) * PEER_KEYS
         + jnp.take_along_axis(i2, pos % PEER_TOPK, axis=-1))
    g = jax.nn.softmax(best, axis=-1)
    K = PEER_HEADS * PEER_TOPK
    nc = T // PEER_CHUNK

    def chunk(args):
        xc, ec, gc = args
        u = u_tab[ec]
        act = jax.nn.gelu(jnp.einsum('cd,ckd->ck', xc, u, preferred_element_type=jnp.float32),
                          approximate=False) * gc
        return jnp.einsum('ck,ckd->cd', act.astype(v_tab.dtype), v_tab[ec])

    y = lax.map(chunk, (ht.reshape(nc, PEER_CHUNK, D), e.reshape(nc, PEER_CHUNK, K),
                        g.reshape(nc, PEER_CHUNK, K)))
    return y.reshape(B, S, D)


def setup_inputs(seed: int = 0) -> dict:
    key = jax.random.key(seed)
    ks = jax.random.split(key, 32)
    f32 = jnp.float32
    L, D = DEPTH, D_MODEL

    def nrm(k, shape, scale):
        return jax.random.normal(k, shape, f32) * scale

    x = nrm(ks[0], (BATCH, SEQ, D), 1.0)
    c = nrm(ks[1], (BATCH, D), 1.0)
    offset = jax.random.randint(ks[2], (BATCH, 1), 0, 4096, dtype=jnp.int32)
    positions = offset + jnp.arange(SEQ, dtype=jnp.int32)[None, :]
    norm1_g = 1.0 + nrm(ks[3], (L, D), 0.02)
    norm2_g = 1.0 + nrm(ks[4], (L, D), 0.02)
    w_ada = nrm(ks[5], (L, D, N_MOD * D), 0.5 * D ** -0.5)
    b_ada = nrm(ks[6], (L, N_MOD * D), 0.01)
    w_in = nrm(ks[7], (L, D, IN_WIDTH), D ** -0.5)
    conv_w = nrm(ks[8], (L, CONV_WIDTH, LRU_WIDTH), CONV_WIDTH ** -0.5)
    conv_b = nrm(ks[9], (L, LRU_WIDTH), 0.01)
    w_a = nrm(ks[10], (L, LRU_BLOCKS, LRU_BLOCK, LRU_BLOCK), LRU_BLOCK ** -0.5)
    b_a = nrm(ks[11], (L, LRU_WIDTH), 0.01)
    w_x = nrm(ks[12], (L, LRU_BLOCKS, LRU_BLOCK, LRU_BLOCK), LRU_BLOCK ** -0.5)
    b_x = nrm(ks[13], (L, LRU_WIDTH), 0.01)
    a_c = jax.random.uniform(ks[14], (L, LRU_WIDTH), f32, 0.9, 0.999)
    a0 = a_c ** (1.0 / LRU_C)
    lru_lambda = jnp.log(a0) - jnp.log1p(-a0)
    q_norm_g = 1.0 + nrm(ks[15], (L, HEAD_DIM), 0.02)
    k_norm_g = 1.0 + nrm(ks[16], (L, HEAD_DIM), 0.02)
    lambda_q1 = nrm(ks[17], (L, HEAD_DIM), 0.1)
    lambda_k1 = nrm(ks[18], (L, HEAD_DIM), 0.1)
    lambda_q2 = nrm(ks[19], (L, HEAD_DIM), 0.1)
    lambda_k2 = nrm(ks[20], (L, HEAD_DIM), 0.1)
    subln_g = 1.0 + nrm(ks[21], (L, V_DIM), 0.02)
    w_lru_out = nrm(ks[22], (L, LRU_WIDTH, D), LRU_WIDTH ** -0.5)
    w_attn_out = nrm(ks[23], (L, ATTN_WIDTH, D), ATTN_WIDTH ** -0.5)
    w_o = nrm(ks[24], (L, D, D), D ** -0.5)
    w_pq = nrm(ks[25], (L, D, PEER_HEADS * PEER_QDIM), D ** -0.5)
    sub_keys = nrm(ks[26], (L, 2, PEER_KEYS, PEER_HALF), PEER_HALF ** -0.5)
    u_tab = nrm(ks[27], (L, PEER_EXPERTS, D), D ** -0.5)
    v_tab = nrm(ks[28], (L, PEER_EXPERTS, D), PEER_HEADS ** -0.5)
    return {'x': x, 'c': c, 'positions': positions, 'norm1_g': norm1_g, 'norm2_g': norm2_g,
            'w_ada': w_ada, 'b_ada': b_ada, 'w_in': w_in, 'conv_w': conv_w, 'conv_b': conv_b,
            'w_a': w_a, 'b_a': b_a, 'w_x': w_x, 'b_x': b_x, 'lru_lambda': lru_lambda,
            'q_norm_g': q_norm_g, 'k_norm_g': k_norm_g, 'lambda_q1': lambda_q1,
            'lambda_k1': lambda_k1, 'lambda_q2': lambda_q2, 'lambda_k2': lambda_k2,
            'subln_g': subln_g, 'w_lru_out': w_lru_out, 'w_attn_out': w_attn_out, 'w_o': w_o,
            'w_pq': w_pq, 'sub_keys': sub_keys, 'u_tab': u_tab, 'v_tab': v_tab}


def reference(x, c, positions, norm1_g, norm2_g, w_ada, b_ada, w_in, conv_w, conv_b,
              w_a, b_a, w_x, b_x, lru_lambda, q_norm_g, k_norm_g, lambda_q1, lambda_k1,
              lambda_q2, lambda_k2, subln_g, w_lru_out, w_attn_out, w_o, w_pq, sub_keys,
              u_tab, v_tab):
    B = x.shape[0]
    c_act = jax.nn.silu(c)
    for l in range(DEPTH):
        lam_init = 0.8 - 0.6 * math.exp(-0.3 * l)
        mod = (c_act @ w_ada[l] + b_ada[l]).reshape(B, N_MOD, 1, D_MODEL)
        shift1, scale1, gate1 = mod[:, 0], mod[:, 1], mod[:, 2]
        shift2, scale2, gate2 = mod[:, 3], mod[:, 4], mod[:, 5]

        h = rms_norm(x, norm1_g[l]) * (1.0 + scale1) + shift1
        z = h @ w_in[l]
        lru_x, lru_gate, q, k, v, mg_lru, mg_attn = partition(z)
        rec = rg_lru_branch(lru_x, conv_w[l], conv_b[l], w_a[l], b_a[l], w_x[l], b_x[l],
                            lru_lambda[l]) * jax.nn.gelu(lru_gate, approximate=False)
        y_lru = rec @ w_lru_out[l]
        att = diff_attention(q, k, v, positions, q_norm_g[l], k_norm_g[l], lambda_q1[l],
                             lambda_k1[l], lambda_q2[l], lambda_k2[l], subln_g[l], lam_init)
        y_att = att @ w_attn_out[l]
        merged = jax.nn.sigmoid(mg_lru) * y_lru + jax.nn.sigmoid(mg_attn) * y_att
        x = x + gate1 * (merged @ w_o[l])

        h2 = rms_norm(x, norm2_g[l]) * (1.0 + scale2) + shift2
        x = x + gate2 * peer_ffn(h2, w_pq[l], sub_keys[l], u_tab[l], v_tab[l])
    return x
```

```python
import functools
import math

import jax
import jax.numpy as jnp
from jax import lax
from jax.experimental import pallas as pl
from jax.experimental.pallas import tpu as pltpu

F32 = jnp.float32
BF16 = jnp.bfloat16

LANES = 128
SUBLANES = 8
VMEM_LIMIT = 56 * 1024 * 1024
EPS = 1e-6
NEG_INF = -1e30
LRU_C = 8.0
ROPE_THETA = 10000.0
PEER_TOPK = 16
N_MOD = 6
INV_SQRT2 = 0.7071067811865476
HIGHEST = lax.Precision.HIGHEST


def _pick(n, pref, align=LANES):
    if n <= pref:
        return n
    t = (pref // align) * align
    while t > align and n % t:
        t -= align
    assert n % t == 0, (n, pref, align)
    return t


def _params(*sem):
    return pltpu.CompilerParams(dimension_semantics=sem, vmem_limit_bytes=VMEM_LIMIT)


def _dot_nt(a, b, precision=None):
    return lax.dot_general(a, b, (((1,), (1,)), ((), ())), precision=precision,
                           preferred_element_type=F32)


def _gelu(x):
    return 0.5 * x * (1.0 + lax.erf(x * INV_SQRT2))


def _ada_kernel(c_ref, w_ref, b_ref, o_ref):
    c = c_ref[...]
    ca = c * jax.nn.sigmoid(c)
    o_ref[0] = jnp.dot(ca, w_ref[0], preferred_element_type=F32, precision=HIGHEST) + b_ref[0]


def _ada(c_pad, w_ada, b_ada):
    L, D, N = w_ada.shape
    tn = _pick(N, 512)
    return pl.pallas_call(
        _ada_kernel,
        grid=(L, N // tn),
        in_specs=[pl.BlockSpec((SUBLANES, D), lambda l, j: (0, 0)),
                  pl.BlockSpec((1, D, tn), lambda l, j: (l, 0, j)),
                  pl.BlockSpec((1, 1, tn), lambda l, j: (l, 0, j))],
        out_specs=pl.BlockSpec((1, SUBLANES, tn), lambda l, j: (l, 0, j)),
        out_shape=jax.ShapeDtypeStruct((L, SUBLANES, N), F32),
        compiler_params=_params("parallel", "parallel"),
        name="adaln",
    )(c_pad, w_ada, b_ada.reshape(L, 1, N))


def _norm_mod_kernel(x_ref, g_ref, sc_ref, sh_ref, o_ref):
    x = x_ref[0]
    ms = jnp.mean(x * x, axis=-1, keepdims=True)
    y = x * lax.rsqrt(ms + EPS) * g_ref[...]
    o_ref[0] = (y * (1.0 + sc_ref[0]) + sh_ref[0]).astype(o_ref.dtype)


def _norm_mod(x, g, scale, shift):
    B, S, D = x.shape
    ts = _pick(S, 256)
    return pl.pallas_call(
        _norm_mod_kernel,
        grid=(B, S // ts),
        in_specs=[pl.BlockSpec((1, ts, D), lambda b, s: (b, s, 0)),
                  pl.BlockSpec((1, D), lambda b, s: (0, 0)),
                  pl.BlockSpec((1, 1, D), lambda b, s: (b, 0, 0)),
                  pl.BlockSpec((1, 1, D), lambda b, s: (b, 0, 0))],
        out_specs=pl.BlockSpec((1, ts, D), lambda b, s: (b, s, 0)),
        out_shape=jax.ShapeDtypeStruct((B, S, D), BF16),
        compiler_params=_params("parallel", "parallel"),
        name="norm_mod",
    )(x, g.reshape(1, D), scale, shift)


def _mm_kernel(a_ref, b_ref, o_ref):
    o_ref[...] = jnp.dot(a_ref[...], b_ref[...], preferred_element_type=F32).astype(o_ref.dtype)


def _matmul(a, b, out_dtype, name):
    M, K = a.shape
    _, N = b.shape
    tm, tn = _pick(M, 1024), _pick(N, 512)
    return pl.pallas_call(
        _mm_kernel,
        grid=(M // tm, N // tn),
        in_specs=[pl.BlockSpec((tm, K), lambda i, j: (i, 0)),
                  pl.BlockSpec((K, tn), lambda i, j: (0, j))],
        out_specs=pl.BlockSpec((tm, tn), lambda i, j: (i, j)),
        out_shape=jax.ShapeDtypeStruct((M, N), out_dtype),
        compiler_params=_params("parallel", "parallel"),
        name=name,
    )(a, b)


def _shift_rows(x, s, fill):
    row = lax.broadcasted_iota(jnp.int32, x.shape, 0)
    return jnp.where(row >= s, pltpu.roll(x, s, 0), fill)


def _linear_scan(a, u):
    n = a.shape[0]
    s = 1
    while s < n:
        u = a * _shift_rows(u, s, 0.0) + u
        a = a * _shift_rows(a, s, 1.0)
        s *= 2
    return a, u


def _lru_kernel(x_ref, gate_ref, cw_ref, cb_ref, wa_ref, ba_ref, wx_ref, bx_ref, lam_ref,
                o_ref, tail_sc, h_sc, *, nblk):
    @pl.when(pl.program_id(2) == 0)
    def _():
        tail_sc[...] = jnp.zeros_like(tail_sc)
        h_sc[...] = jnp.zeros_like(h_sc)

    ts = x_ref.shape[1]
    row8 = lax.broadcasted_iota(jnp.int32, (SUBLANES, LANES), 0)
    for j in range(nblk):
        sl = slice(j * LANES, (j + 1) * LANES)
        x = x_ref[0, :, sl].astype(F32)
        tail = tail_sc[:, sl]
        cw = cw_ref[:, sl]
        nk = cw.shape[0]
        xc = cb_ref[:, sl] + cw[nk - 1:nk] * x
        for d in range(1, nk):
            xr = pltpu.roll(x, d, 0)
            head = jnp.where(row8 < d, pltpu.roll(tail, d, 0), xr[:SUBLANES])
            xd = jnp.concatenate([head, xr[SUBLANES:]], axis=0)
            xc = xc + cw[nk - 1 - d:nk - d] * xd
        tail_sc[:, sl] = x[ts - SUBLANES:]

        xcb = xc.astype(BF16)
        r = jax.nn.sigmoid(jnp.dot(xcb, wa_ref[j], preferred_element_type=F32) + ba_ref[:, sl])
        i = jax.nn.sigmoid(jnp.dot(xcb, wx_ref[j], preferred_element_type=F32) + bx_ref[:, sl])
        neg_lam = -lam_ref[:, sl]
        softplus = jnp.maximum(neg_lam, 0.0) + jnp.log1p(jnp.exp(-jnp.abs(neg_lam)))
        log_a = (-LRU_C) * r * softplus
        a = jnp.exp(log_a)
        u = jnp.sqrt(1.0 - a * a) * (i * xc)
        acum, h = _linear_scan(a, u)
        h = h + acum * h_sc[:, sl]
        h_sc[:, sl] = h[ts - 1:ts]
        gate = gate_ref[0, :, sl].astype(F32)
        o_ref[0, :, sl] = (h * _gelu(gate)).astype(o_ref.dtype)


def _lru(z, conv_w, conv_b, w_a, b_a, w_x, b_x, lru_lambda, lru_w):
    B, S, _ = z.shape
    nblocks = w_a.shape[0]
    assert w_a.shape[1] == LANES and lru_w == nblocks * LANES
    cw = _pick(lru_w, 512)
    nblk = cw // LANES
    ts = _pick(S, 512, SUBLANES)
    ncb = lru_w // cw
    row = lambda v: v.reshape(1, lru_w)
    vec_spec = pl.BlockSpec((1, cw), lambda b, c, s: (0, c))
    return pl.pallas_call(
        functools.partial(_lru_kernel, nblk=nblk),
        grid=(B, ncb, S // ts),
        in_specs=[pl.BlockSpec((1, ts, cw), lambda b, c, s: (b, s, c)),
                  pl.BlockSpec((1, ts, cw), lambda b, c, s: (b, s, c + ncb)),
                  pl.BlockSpec((conv_w.shape[0], cw), lambda b, c, s: (0, c)),
                  vec_spec,
                  pl.BlockSpec((nblk, LANES, LANES), lambda b, c, s: (c, 0, 0)),
                  vec_spec,
                  pl.BlockSpec((nblk, LANES, LANES), lambda b, c, s: (c, 0, 0)),
                  vec_spec, vec_spec],
        out_specs=pl.BlockSpec((1, ts, cw), lambda b, c, s: (b, s, c)),
        out_shape=jax.ShapeDtypeStruct((B, S, lru_w), BF16),
        scratch_shapes=[pltpu.VMEM((SUBLANES, cw), F32), pltpu.VMEM((1, cw), F32)],
        compiler_params=_params("parallel", "parallel", "arbitrary"),
        name="rg_lru",
    )(z, z, conv_w, row(conv_b), w_a.astype(BF16), row(b_a), w_x.astype(BF16), row(b_x),
      row(lru_lambda))


def _rope_kernel(pos_ref, f_ref, sg_ref, cos_ref, sin_ref):
    ang = pos_ref[0].astype(F32) * f_ref[...]
    cos_ref[0] = jnp.cos(ang)
    sin_ref[0] = jnp.sin(ang) * sg_ref[...]


def _rope_tables(positions, head_dim):
    B, S = positions.shape
    half = head_dim // 2
    inv_freq = ROPE_THETA ** (-jnp.arange(0, head_dim, 2, dtype=F32) / head_dim)
    freq_row = jnp.tile(inv_freq, LANES // half).reshape(1, LANES)
    sign_row = jnp.tile(jnp.concatenate([-jnp.ones(half, F32), jnp.ones(half, F32)]),
                        LANES // head_dim).reshape(1, LANES)
    ts = _pick(S, 1024, SUBLANES)
    tab = jax.ShapeDtypeStruct((B, S, LANES), F32)
    return pl.pallas_call(
        _rope_kernel,
        grid=(B, S // ts),
        in_specs=[pl.BlockSpec((1, ts, 1), lambda b, s: (b, s, 0)),
                  pl.BlockSpec((1, LANES), lambda b, s: (0, 0)),
                  pl.BlockSpec((1, LANES), lambda b, s: (0, 0))],
        out_specs=[pl.BlockSpec((1, ts, LANES), lambda b, s: (b, s, 0))] * 2,
        out_shape=[tab, tab],
        compiler_params=_params("parallel", "parallel"),
        name="rope_tables",
    )(positions.reshape(B, S, 1), freq_row, sign_row)


def _qk_prep_kernel(q_ref, k_ref, cos_ref, sin_ref, gq_ref, gk_ref, p_ref, qo_ref, ko_ref,
                    *, nheads, head_dim, q_scale):
    cos = cos_ref[0]
    sin = sin_ref[0]
    pmat = p_ref[...]
    lane = lax.broadcasted_iota(jnp.int32, cos.shape, 1)
    first_half = (lane % head_dim) < (head_dim // 2)

    def prep(t, g):
        ms = jnp.dot(t * t, pmat, preferred_element_type=F32, precision=HIGHEST)
        tn = t * lax.rsqrt(ms + EPS) * g
        swapped = jnp.where(first_half, pltpu.roll(tn, LANES - head_dim // 2, 1),
                            pltpu.roll(tn, head_dim // 2, 1))
        return tn * cos + swapped * sin

    for h in range(nheads):
        sl = slice(h * LANES, (h + 1) * LANES)
        qo_ref[0, :, sl] = (prep(q_ref[0, :, sl].astype(F32), gq_ref[...]) * q_scale).astype(BF16)
        ko_ref[0, :, sl] = prep(k_ref[0, :, sl].astype(F32), gk_ref[...]).astype(BF16)


def _qk_prep(z, cos_t, sin_t, q_norm_g, k_norm_g, q_col, k_col, attn_w, head_dim):
    B, S, _ = z.shape
    assert 2 * head_dim == LANES
    nheads = attn_w // LANES
    ts = _pick(S, 256, SUBLANES)
    qb, kb = q_col // attn_w, k_col // attn_w
    lane = jnp.arange(LANES)
    pmat = (lane[:, None] // head_dim == lane[None, :] // head_dim).astype(F32) / head_dim
    out = jax.ShapeDtypeStruct((B, S, attn_w), BF16)
    tab_spec = pl.BlockSpec((1, ts, LANES), lambda b, s: (b, s, 0))
    row_spec = pl.BlockSpec((1, LANES), lambda b, s: (0, 0))
    return pl.pallas_call(
        functools.partial(_qk_prep_kernel, nheads=nheads, head_dim=head_dim,
                          q_scale=head_dim ** -0.5),
        grid=(B, S // ts),
        in_specs=[pl.BlockSpec((1, ts, attn_w), lambda b, s: (b, s, qb)),
                  pl.BlockSpec((1, ts, attn_w), lambda b, s: (b, s, kb)),
                  tab_spec, tab_spec, row_spec, row_spec,
                  pl.BlockSpec((LANES, LANES), lambda b, s: (0, 0))],
        out_specs=[pl.BlockSpec((1, ts, attn_w), lambda b, s: (b, s, 0))] * 2,
        out_shape=[out, out],
        compiler_params=_params("parallel", "parallel"),
        name="qk_prep",
    )(z, z, cos_t, sin_t, jnp.tile(q_norm_g, 2).reshape(1, LANES),
      jnp.tile(k_norm_g, 2).reshape(1, LANES), pmat)


def _attn_kernel(q_ref, k_ref, v_ref, lams_ref, g_ref, o_ref, qq_sc, m_sc, l_sc, acc_sc,
                 *, tq, tk, head_dim, lam_init):
    qi = pl.program_id(2)
    ki = pl.program_id(3)

    @pl.when(ki == 0)
    def _():
        q = q_ref[0]
        lane = lax.broadcasted_iota(jnp.int32, q.shape, 1)
        zero = jnp.zeros_like(q)
        qq_sc[0:tq] = jnp.where(lane < head_dim, q, zero)
        qq_sc[tq:2 * tq] = jnp.where(lane >= head_dim, q, zero)
        m_sc[...] = jnp.full_like(m_sc, NEG_INF)
        l_sc[...] = jnp.zeros_like(l_sc)
        acc_sc[...] = jnp.zeros_like(acc_sc)

    @pl.when(ki <= qi)
    def _():
        s = _dot_nt(qq_sc[...], k_ref[0])
        row = lax.broadcasted_iota(jnp.int32, s.shape, 0)
        col = lax.broadcasted_iota(jnp.int32, s.shape, 1)
        q_idx = jnp.where(row >= tq, row - tq, row) + qi * tq
        s = jnp.where(col + ki * tk <= q_idx, s, NEG_INF)
        m_prev = m_sc[...]
        m_new = jnp.maximum(m_prev, jnp.max(s, axis=1, keepdims=True))
        alpha = jnp.exp(m_prev - m_new)
        p = jnp.exp(s - jnp.concatenate([m_new] * (tk // LANES), axis=1))
        l_sc[...] = alpha * l_sc[...] + jnp.sum(p, axis=1, keepdims=True)
        acc_sc[...] = alpha * acc_sc[...] + jnp.dot(p.astype(BF16), v_ref[0],
                                                    preferred_element_type=F32)
        m_sc[...] = m_new

    @pl.when(ki == qi)
    def _():
        o = acc_sc[...] / l_sc[...]
        lv = lams_ref[...]
        lam = (jnp.exp(jnp.sum(lv[0:1] * lv[1:2], axis=1, keepdims=True))
               - jnp.exp(jnp.sum(lv[2:3] * lv[3:4], axis=1, keepdims=True)) + lam_init)
        d = o[0:tq] - lam * o[tq:2 * tq]
        ms = jnp.mean(d * d, axis=-1, keepdims=True)
        y = d * lax.rsqrt(ms + EPS) * g_ref[...] * (1.0 - lam_init)
        o_ref[0] = y.astype(o_ref.dtype)


def _attention(qn, kn, z, lams, subln_g, v_col, head_dim, lam_init):
    B, S, attn_w = qn.shape
    nheads = attn_w // LANES
    tq = tk = _pick(S, 512)
    vb = v_col // LANES
    return pl.pallas_call(
        functools.partial(_attn_kernel, tq=tq, tk=tk, head_dim=head_dim, lam_init=lam_init),
        grid=(B, nheads, S // tq, S // tk),
        in_specs=[pl.BlockSpec((1, tq, LANES), lambda b, h, i, j: (b, i, h)),
                  pl.BlockSpec((1, tk, LANES), lambda b, h, i, j: (b, jnp.minimum(i, j), h)),
                  pl.BlockSpec((1, tk, LANES), lambda b, h, i, j: (b, jnp.minimum(i, j), vb + h)),
                  pl.BlockSpec((4, head_dim), lambda b, h, i, j: (0, 0)),
                  pl.BlockSpec((1, LANES), lambda b, h, i, j: (0, 0))],
        out_specs=pl.BlockSpec((1, tq, LANES), lambda b, h, i, j: (b, i, h)),
        out_shape=jax.ShapeDtypeStruct((B, S, attn_w), BF16),
        scratch_shapes=[pltpu.VMEM((2 * tq, LANES), BF16), pltpu.VMEM((2 * tq, LANES), F32),
                        pltpu.VMEM((2 * tq, LANES), F32), pltpu.VMEM((2 * tq, LANES), F32)],
        compiler_params=_params("parallel", "parallel", "parallel", "arbitrary"),
        name="diff_attention",
    )(qn, kn, z, lams, subln_g.reshape(1, LANES))


def _merge_kernel(rec_ref, att_ref, wl_ref, wa_ref, gl_ref, ga_ref, o_ref):
    yl = jnp.dot(rec_ref[...], wl_ref[...], preferred_element_type=F32)
    ya = jnp.dot(att_ref[...], wa_ref[...], preferred_element_type=F32)
    gl = jax.nn.sigmoid(gl_ref[...].astype(F32))
    ga = jax.nn.sigmoid(ga_ref[...].astype(F32))
    o_ref[...] = (gl * yl + ga * ya).astype(o_ref.dtype)


def _merge(rec, att, w_lru_out, w_attn_out, z2, gl_col, ga_col):
    T, lru_w = rec.shape
    attn_w = att.shape[1]
    D = w_lru_out.shape[1]
    tm, tn = _pick(T, 1024), _pick(math.gcd(D, gl_col, ga_col), 512)
    glb, gab = gl_col // tn, ga_col // tn
    return pl.pallas_call(
        _merge_kernel,
        grid=(T // tm, D // tn),
        in_specs=[pl.BlockSpec((tm, lru_w), lambda i, j: (i, 0)),
                  pl.BlockSpec((tm, attn_w), lambda i, j: (i, 0)),
                  pl.BlockSpec((lru_w, tn), lambda i, j: (0, j)),
                  pl.BlockSpec((attn_w, tn), lambda i, j: (0, j)),
                  pl.BlockSpec((tm, tn), lambda i, j: (i, glb + j)),
                  pl.BlockSpec((tm, tn), lambda i, j: (i, gab + j))],
        out_specs=pl.BlockSpec((tm, tn), lambda i, j: (i, j)),
        out_shape=jax.ShapeDtypeStruct((T, D), BF16),
        compiler_params=_params("parallel", "parallel"),
        name="merge",
    )(rec, att, w_lru_out, w_attn_out, z2, z2)


def _proj_res_kernel(a_ref, w_ref, x_ref, g_ref, o_ref):
    y = jnp.dot(a_ref[...], w_ref[...], preferred_element_type=F32)
    o_ref[...] = x_ref[...] + g_ref[0] * y


def _proj_residual(a, w, x2, gate, seq):
    T, K = a.shape
    D = w.shape[1]
    tm, tn = _pick(math.gcd(T, seq), 1024), _pick(D, 512)
    per_b = seq // tm
    return pl.pallas_call(
        _proj_res_kernel,
        grid=(T // tm, D // tn),
        in_specs=[pl.BlockSpec((tm, K), lambda i, j: (i, 0)),
                  pl.BlockSpec((K, tn), lambda i, j: (0, j)),
                  pl.BlockSpec((tm, tn), lambda i, j: (i, j)),
                  pl.BlockSpec((1, 1, tn), lambda i, j: (i // per_b, 0, j))],
        out_specs=pl.BlockSpec((tm, tn), lambda i, j: (i, j)),
        out_shape=jax.ShapeDtypeStruct((T, D), F32),
        compiler_params=_params("parallel", "parallel"),
        name="proj_residual",
    )(a, w, x2, gate)


def _topk_rows(s, extras, k):
    nrows = s.shape[0]
    rowf = lax.broadcasted_iota(jnp.int32, s.shape, 0).astype(F32)
    vals, picked = [], [[] for _ in extras]
    idxs = []
    for _ in range(k):
        m = jnp.max(s, axis=0, keepdims=True)
        idx = jnp.min(jnp.where(s == m, rowf, float(nrows)), axis=0, keepdims=True)
        sel = rowf == idx
        vals.append(m)
        idxs.append(idx)
        for lst, arr in zip(picked, extras):
            lst.append(jnp.max(jnp.where(sel, arr, -1.0), axis=0, keepdims=True))
        s = jnp.where(sel, -jnp.inf, s)
    return vals, idxs, picked


def _stack_rows(rows):
    k, n = len(rows), rows[0].shape[1]
    row = lax.broadcasted_iota(jnp.int32, (k, n), 0)
    out = jnp.zeros((k, n), F32)
    for r, v in enumerate(rows):
        out = jnp.where(row == r, v, out)
    return out


def _topk_kernel(q_ref, keys_ref, ia_ref, ib_ref, g_ref, *, k):
    q = q_ref[...]
    half = q.shape[1] // 2
    s1 = _dot_nt(keys_ref[0], q[:, :half], HIGHEST)
    s2 = _dot_nt(keys_ref[1], q[:, half:], HIGHEST)
    v1, i1, _ = _topk_rows(s1, (), k)
    v2, i2, _ = _topk_rows(s2, (), k)
    v2s, i2s = _stack_rows(v2), _stack_rows(i2)
    n = q.shape[0]
    cand = jnp.concatenate([v1[i] + v2s for i in range(k)], axis=0)
    ca = jnp.concatenate([jnp.broadcast_to(i1[i], (k, n)) for i in range(k)], axis=0)
    cb = jnp.concatenate([i2s] * k, axis=0)
    best, _, (a_sel, b_sel) = _topk_rows(cand, (ca, cb), k)
    bests = _stack_rows(best)
    e = jnp.exp(bests - best[0])
    g_ref[...] = e / jnp.sum(e, axis=0, keepdims=True)
    ia_ref[...] = _stack_rows(a_sel)
    ib_ref[...] = _stack_rows(b_sel)


def _peer_topk(q, sub_keys, nheads):
    T = q.shape[0]
    nkeys, half = sub_keys.shape[1], sub_keys.shape[2]
    assert half == LANES and nkeys == LANES
    k = PEER_TOPK
    tt = _pick(T, 256)
    out = jax.ShapeDtypeStruct((nheads * k, T), F32)
    out_spec = pl.BlockSpec((k, tt), lambda i, h: (h, i))
    return pl.pallas_call(
        functools.partial(_topk_kernel, k=k),
        grid=(T // tt, nheads),
        in_specs=[pl.BlockSpec((tt, 2 * half), lambda i, h: (i, h)),
                  pl.BlockSpec((2, nkeys, half), lambda i, h: (0, 0, 0))],
        out_specs=[out_spec] * 3,
        out_shape=[out] * 3,
        compiler_params=_params("parallel", "parallel"),
        name="peer_topk",
    )(q, sub_keys)


def _gates_kernel(ia_ref, ib_ref, g_ref, o_ref, a_sc, b_sc, w_sc):
    a_sc[...] = ia_ref[...].T
    b_sc[...] = ib_ref[...].T
    w_sc[...] = g_ref[...].T
    nslots = ia_ref.shape[0]
    nkeys = o_ref.shape[1]
    keyf = lax.broadcasted_iota(jnp.int32, (nkeys, nslots), 0).astype(F32)

    def body(t, carry):
        a_row = a_sc[pl.ds(t, 1), :]
        b_row = b_sc[pl.ds(t, 1), :]
        w_row = w_sc[pl.ds(t, 1), :]
        lhs = jnp.where(keyf == a_row, w_row, 0.0).astype(BF16)
        rhs = jnp.where(keyf == b_row, 1.0, 0.0).astype(BF16)
        o_ref[t] = _dot_nt(lhs, rhs).astype(o_ref.dtype)
        return carry

    lax.fori_loop(0, o_ref.shape[0], body, 0)


def _peer_gates(ia, ib, g, nkeys):
    nslots, T = ia.shape
    tg = _pick(T, 128)
    in_spec = pl.BlockSpec((nslots, tg), lambda i: (0, i))
    return pl.pallas_call(
        _gates_kernel,
        grid=(T // tg,),
        in_specs=[in_spec] * 3,
        out_specs=pl.BlockSpec((tg, nkeys, nkeys), lambda i: (i, 0, 0)),
        out_shape=jax.ShapeDtypeStruct((T, nkeys, nkeys), BF16),
        scratch_shapes=[pltpu.VMEM((tg, nslots), F32)] * 3,
        compiler_params=_params("parallel"),
        name="peer_gates",
    )(ia, ib, g)


def _peer_kernel(h_ref, u_ref, v_ref, g_ref, o_ref):
    @pl.when(pl.program_id(1) == 0)
    def _():
        o_ref[...] = jnp.zeros_like(o_ref)

    s = _dot_nt(h_ref[...], u_ref[...])
    w = (_gelu(s) * g_ref[...].astype(F32)).astype(BF16)
    o_ref[...] += jnp.dot(w, v_ref[...], preferred_element_type=F32)


def _peer_dense(h2, u_tab, v_tab, gates):
    T, D = h2.shape
    E = u_tab.shape[0]
    tm, te = _pick(T, 512), _pick(E, 512)
    return pl.pallas_call(
        _peer_kernel,
        grid=(T // tm, E // te),
        in_specs=[pl.BlockSpec((tm, D), lambda i, j: (i, 0)),
                  pl.BlockSpec((te, D), lambda i, j: (j, 0)),
                  pl.BlockSpec((te, D), lambda i, j: (j, 0)),
                  pl.BlockSpec((tm, te), lambda i, j: (i, j))],
        out_specs=pl.BlockSpec((tm, D), lambda i, j: (i, 0)),
        out_shape=jax.ShapeDtypeStruct((T, D), F32),
        compiler_params=_params("parallel", "arbitrary"),
        name="peer_dense",
    )(h2, u_tab, v_tab, gates)


def _residual_kernel(x_ref, y_ref, g_ref, o_ref):
    o_ref[0] = x_ref[0] + g_ref[0] * y_ref[0]


def _residual(x, y, gate):
    B, S, D = x.shape
    ts = _pick(S, 256, SUBLANES)
    blk = pl.BlockSpec((1, ts, D), lambda b, s: (b, s, 0))
    return pl.pallas_call(
        _residual_kernel,
        grid=(B, S // ts),
        in_specs=[blk, blk, pl.BlockSpec((1, 1, D), lambda b, s: (b, 0, 0))],
        out_specs=blk,
        out_shape=jax.ShapeDtypeStruct((B, S, D), F32),
        compiler_params=_params("parallel", "parallel"),
        name="residual",
    )(x, y, gate)


def kernel(x, c, positions, norm1_g, norm2_g, w_ada, b_ada, w_in, conv_w, conv_b, w_a, b_a, w_x,
           b_x, lru_lambda, q_norm_g, k_norm_g, lambda_q1, lambda_k1, lambda_q2, lambda_k2,
           subln_g, w_lru_out, w_attn_out, w_o, w_pq, sub_keys, u_tab, v_tab):
    B, S, D = x.shape
    T = B * S
    depth = w_in.shape[0]
    lru_w = conv_w.shape[-1]
    attn_w = w_attn_out.shape[1]
    head_dim = q_norm_g.shape[-1]
    nkeys = sub_keys.shape[2]
    peer_heads = w_pq.shape[2] // (2 * sub_keys.shape[3])
    assert B <= SUBLANES and subln_g.shape[-1] == LANES
    q_col = 2 * lru_w
    k_col = q_col + attn_w
    v_col = k_col + attn_w
    gl_col = v_col + attn_w
    ga_col = gl_col + D

    c_pad = jnp.pad(c, ((0, SUBLANES - B), (0, 0)))
    mod = _ada(c_pad, w_ada, b_ada)[:, :B].reshape(depth, B, N_MOD, 1, D)
    cos_t, sin_t = _rope_tables(positions, head_dim)

    for l in range(depth):
        lam_init = 0.8 - 0.6 * math.exp(-0.3 * l)
        shift1, scale1, gate1 = mod[l, :, 0], mod[l, :, 1], mod[l, :, 2]
        shift2, scale2, gate2 = mod[l, :, 3], mod[l, :, 4], mod[l, :, 5]

        h = _norm_mod(x, norm1_g[l], scale1, shift1)
        z2 = _matmul(h.reshape(T, D), w_in[l].astype(BF16), BF16, "in_proj")
        z = z2.reshape(B, S, -1)
        rec = _lru(z, conv_w[l], conv_b[l], w_a[l], b_a[l], w_x[l], b_x[l], lru_lambda[l], lru_w)
        qn, kn = _qk_prep(z, cos_t, sin_t, q_norm_g[l], k_norm_g[l], q_col, k_col, attn_w, head_dim)
        lams = jnp.stack([lambda_q1[l], lambda_k1[l], lambda_q2[l], lambda_k2[l]])
        att = _attention(qn, kn, z, lams, subln_g[l], v_col, head_dim, lam_init)
        merged = _merge(rec.reshape(T, lru_w), att.reshape(T, attn_w), w_lru_out[l].astype(BF16),
                        w_attn_out[l].astype(BF16), z2, gl_col, ga_col)
        x = _proj_residual(merged, w_o[l].astype(BF16), x.reshape(T, D), gate1, S).reshape(B, S, D)

        h2 = _norm_mod(x, norm2_g[l], scale2, shift2).reshape(T, D)
        pq = _matmul(h2, w_pq[l].astype(BF16), F32, "peer_query")
        ia, ib, g = _peer_topk(pq, sub_keys[l], peer_heads)
        gates = _peer_gates(ia, ib, g, nkeys).reshape(T, nkeys * nkeys)
        y = _peer_dense(h2, u_tab[l].astype(BF16), v_tab[l].astype(BF16), gates)
        x = _residual(x, y.reshape(B, S, D), gate2)
    return x
```

```python
import functools
import math

import jax
import jax.numpy as jnp
from jax import lax
from jax.experimental import pallas as pl
from jax.experimental.pallas import tpu as pltpu

F32 = jnp.float32
BF16 = jnp.bfloat16

LANES = 128
SUBLANES = 8
VMEM_LIMIT = 56 * 1024 * 1024
EPS = 1e-6
NEG_INF = -1e30
LRU_C = 8.0
ROPE_THETA = 10000.0
PEER_TOPK = 16
N_MOD = 6
INV_SQRT2 = 0.7071067811865476
LOG2_E = 1.4426950408889634
ATTN_BLOCK = 1024
ATTN_ROW_CHUNK = 256
HIGHEST = lax.Precision.HIGHEST


def _pick(n, pref, align=LANES):
    if n <= pref:
        return n
    t = (pref // align) * align
    while t > align and n % t:
        t -= align
    assert n % t == 0, (n, pref, align)
    return t


def _params(*sem):
    return pltpu.CompilerParams(dimension_semantics=sem, vmem_limit_bytes=VMEM_LIMIT)


def _dot_nt(a, b, precision=None):
    return lax.dot_general(a, b, (((1,), (1,)), ((), ())), precision=precision,
                           preferred_element_type=F32)


def _gelu(x):
    return 0.5 * x * (1.0 + lax.erf(x * INV_SQRT2))


def _ada_kernel(c_ref, w_ref, b_ref, o_ref):
    c = c_ref[...]
    ca = c * jax.nn.sigmoid(c)
    o_ref[0] = jnp.dot(ca, w_ref[0], preferred_element_type=F32, precision=HIGHEST) + b_ref[0]


def _ada(c_pad, w_ada, b_ada):
    L, D, N = w_ada.shape
    tn = _pick(N, 512)
    return pl.pallas_call(
        _ada_kernel,
        grid=(L, N // tn),
        in_specs=[pl.BlockSpec((SUBLANES, D), lambda l, j: (0, 0)),
                  pl.BlockSpec((1, D, tn), lambda l, j: (l, 0, j)),
                  pl.BlockSpec((1, 1, tn), lambda l, j: (l, 0, j))],
        out_specs=pl.BlockSpec((1, SUBLANES, tn), lambda l, j: (l, 0, j)),
        out_shape=jax.ShapeDtypeStruct((L, SUBLANES, N), F32),
        compiler_params=_params("parallel", "parallel"),
        name="adaln",
    )(c_pad, w_ada, b_ada.reshape(L, 1, N))


def _norm_mod_kernel(x_ref, g_ref, sc_ref, sh_ref, o_ref):
    x = x_ref[0]
    ms = jnp.mean(x * x, axis=-1, keepdims=True)
    y = x * lax.rsqrt(ms + EPS) * g_ref[...]
    o_ref[0] = (y * (1.0 + sc_ref[0]) + sh_ref[0]).astype(o_ref.dtype)


def _norm_mod(x, g, scale, shift):
    B, S, D = x.shape
    ts = _pick(S, 256)
    return pl.pallas_call(
        _norm_mod_kernel,
        grid=(B, S // ts),
        in_specs=[pl.BlockSpec((1, ts, D), lambda b, s: (b, s, 0)),
                  pl.BlockSpec((1, D), lambda b, s: (0, 0)),
                  pl.BlockSpec((1, 1, D), lambda b, s: (b, 0, 0)),
                  pl.BlockSpec((1, 1, D), lambda b, s: (b, 0, 0))],
        out_specs=pl.BlockSpec((1, ts, D), lambda b, s: (b, s, 0)),
        out_shape=jax.ShapeDtypeStruct((B, S, D), BF16),
        compiler_params=_params("parallel", "parallel"),
        name="norm_mod",
    )(x, g.reshape(1, D), scale, shift)


def _mm_kernel(a_ref, b_ref, o_ref):
    o_ref[...] = jnp.dot(a_ref[...], b_ref[...], preferred_element_type=F32).astype(o_ref.dtype)


def _matmul(a, b, out_dtype, name):
    M, K = a.shape
    _, N = b.shape
    tm, tn = _pick(M, 1024), _pick(N, 512)
    return pl.pallas_call(
        _mm_kernel,
        grid=(M // tm, N // tn),
        in_specs=[pl.BlockSpec((tm, K), lambda i, j: (i, 0)),
                  pl.BlockSpec((K, tn), lambda i, j: (0, j))],
        out_specs=pl.BlockSpec((tm, tn), lambda i, j: (i, j)),
        out_shape=jax.ShapeDtypeStruct((M, N), out_dtype),
        compiler_params=_params("parallel", "parallel"),
        name=name,
    )(a, b)


def _shift_rows(x, s, fill):
    row = lax.broadcasted_iota(jnp.int32, x.shape, 0)
    return jnp.where(row >= s, pltpu.roll(x, s, 0), fill)


def _linear_scan(a, u):
    n = a.shape[0]
    s = 1
    while s < n:
        u = a * _shift_rows(u, s, 0.0) + u
        a = a * _shift_rows(a, s, 1.0)
        s *= 2
    return a, u


def _lru_kernel(x_ref, gate_ref, cw_ref, cb_ref, wa_ref, ba_ref, wx_ref, bx_ref, lam_ref,
                o_ref, tail_sc, h_sc, *, nblk):
    @pl.when(pl.program_id(2) == 0)
    def _():
        tail_sc[...] = jnp.zeros_like(tail_sc)
        h_sc[...] = jnp.zeros_like(h_sc)

    ts = x_ref.shape[1]
    row8 = lax.broadcasted_iota(jnp.int32, (SUBLANES, LANES), 0)
    for j in range(nblk):
        sl = slice(j * LANES, (j + 1) * LANES)
        x = x_ref[0, :, sl].astype(F32)
        tail = tail_sc[:, sl]
        cw = cw_ref[:, sl]
        nk = cw.shape[0]
        xc = cb_ref[:, sl] + cw[nk - 1:nk] * x
        for d in range(1, nk):
            xr = pltpu.roll(x, d, 0)
            head = jnp.where(row8 < d, pltpu.roll(tail, d, 0), xr[:SUBLANES])
            xd = jnp.concatenate([head, xr[SUBLANES:]], axis=0)
            xc = xc + cw[nk - 1 - d:nk - d] * xd
        tail_sc[:, sl] = x[ts - SUBLANES:]

        xcb = xc.astype(BF16)
        r = jax.nn.sigmoid(jnp.dot(xcb, wa_ref[j], preferred_element_type=F32) + ba_ref[:, sl])
        i = jax.nn.sigmoid(jnp.dot(xcb, wx_ref[j], preferred_element_type=F32) + bx_ref[:, sl])
        neg_lam = -lam_ref[:, sl]
        softplus = jnp.maximum(neg_lam, 0.0) + jnp.log1p(jnp.exp(-jnp.abs(neg_lam)))
        log_a = (-LRU_C) * r * softplus
        a = jnp.exp(log_a)
        u = jnp.sqrt(1.0 - a * a) * (i * xc)
        acum, h = _linear_scan(a, u)
        h = h + acum * h_sc[:, sl]
        h_sc[:, sl] = h[ts - 1:ts]
        gate = gate_ref[0, :, sl].astype(F32)
        o_ref[0, :, sl] = (h * _gelu(gate)).astype(o_ref.dtype)


def _lru(z, conv_w, conv_b, w_a, b_a, w_x, b_x, lru_lambda, lru_w):
    B, S, _ = z.shape
    nblocks = w_a.shape[0]
    assert w_a.shape[1] == LANES and lru_w == nblocks * LANES
    cw = _pick(lru_w, 512)
    nblk = cw // LANES
    ts = _pick(S, 512, SUBLANES)
    ncb = lru_w // cw
    row = lambda v: v.reshape(1, lru_w)
    vec_spec = pl.BlockSpec((1, cw), lambda b, c, s: (0, c))
    return pl.pallas_call(
        functools.partial(_lru_kernel, nblk=nblk),
        grid=(B, ncb, S // ts),
        in_specs=[pl.BlockSpec((1, ts, cw), lambda b, c, s: (b, s, c)),
                  pl.BlockSpec((1, ts, cw), lambda b, c, s: (b, s, c + ncb)),
                  pl.BlockSpec((conv_w.shape[0], cw), lambda b, c, s: (0, c)),
                  vec_spec,
                  pl.BlockSpec((nblk, LANES, LANES), lambda b, c, s: (c, 0, 0)),
                  vec_spec,
                  pl.BlockSpec((nblk, LANES, LANES), lambda b, c, s: (c, 0, 0)),
                  vec_spec, vec_spec],
        out_specs=pl.BlockSpec((1, ts, cw), lambda b, c, s: (b, s, c)),
        out_shape=jax.ShapeDtypeStruct((B, S, lru_w), BF16),
        scratch_shapes=[pltpu.VMEM((SUBLANES, cw), F32), pltpu.VMEM((1, cw), F32)],
        compiler_params=_params("parallel", "parallel", "arbitrary"),
        name="rg_lru",
    )(z, z, conv_w, row(conv_b), w_a.astype(BF16), row(b_a), w_x.astype(BF16), row(b_x),
      row(lru_lambda))


def _rope_kernel(pos_ref, f_ref, sg_ref, cos_ref, sin_ref):
    ang = pos_ref[0].astype(F32) * f_ref[...]
    cos_ref[0] = jnp.cos(ang)
    sin_ref[0] = jnp.sin(ang) * sg_ref[...]


def _rope_tables(positions, head_dim):
    B, S = positions.shape
    half = head_dim // 2
    inv_freq = ROPE_THETA ** (-jnp.arange(0, head_dim, 2, dtype=F32) / head_dim)
    freq_row = jnp.tile(inv_freq, LANES // half).reshape(1, LANES)
    sign_row = jnp.tile(jnp.concatenate([-jnp.ones(half, F32), jnp.ones(half, F32)]),
                        LANES // head_dim).reshape(1, LANES)
    ts = _pick(S, 1024, SUBLANES)
    tab = jax.ShapeDtypeStruct((B, S, LANES), F32)
    return pl.pallas_call(
        _rope_kernel,
        grid=(B, S // ts),
        in_specs=[pl.BlockSpec((1, ts, 1), lambda b, s: (b, s, 0)),
                  pl.BlockSpec((1, LANES), lambda b, s: (0, 0)),
                  pl.BlockSpec((1, LANES), lambda b, s: (0, 0))],
        out_specs=[pl.BlockSpec((1, ts, LANES), lambda b, s: (b, s, 0))] * 2,
        out_shape=[tab, tab],
        compiler_params=_params("parallel", "parallel"),
        name="rope_tables",
    )(positions.reshape(B, S, 1), freq_row, sign_row)


def _qk_prep_kernel(q_ref, k_ref, cos_ref, sin_ref, gq_ref, gk_ref, p_ref, qo_ref, ko_ref,
                    *, nheads, head_dim, q_scale):
    cos = cos_ref[0]
    sin = sin_ref[0]
    pmat = p_ref[...]
    lane = lax.broadcasted_iota(jnp.int32, cos.shape, 1)
    first_half = (lane % head_dim) < (head_dim // 2)

    def prep(t, g):
        ms = jnp.dot(t * t, pmat, preferred_element_type=F32, precision=HIGHEST)
        tn = t * lax.rsqrt(ms + EPS) * g
        swapped = jnp.where(first_half, pltpu.roll(tn, LANES - head_dim // 2, 1),
                            pltpu.roll(tn, head_dim // 2, 1))
        return tn * cos + swapped * sin

    for h in range(nheads):
        sl = slice(h * LANES, (h + 1) * LANES)
        qo_ref[0, :, sl] = (prep(q_ref[0, :, sl].astype(F32), gq_ref[...]) * q_scale).astype(BF16)
        ko_ref[0, :, sl] = prep(k_ref[0, :, sl].astype(F32), gk_ref[...]).astype(BF16)


def _qk_prep(z, cos_t, sin_t, q_norm_g, k_norm_g, q_col, k_col, attn_w, head_dim):
    B, S, _ = z.shape
    assert 2 * head_dim == LANES
    nheads = attn_w // LANES
    ts = _pick(S, 256, SUBLANES)
    qb, kb = q_col // attn_w, k_col // attn_w
    lane = jnp.arange(LANES)
    pmat = (lane[:, None] // head_dim == lane[None, :] // head_dim).astype(F32) / head_dim
    out = jax.ShapeDtypeStruct((B, S, attn_w), BF16)
    tab_spec = pl.BlockSpec((1, ts, LANES), lambda b, s: (b, s, 0))
    row_spec = pl.BlockSpec((1, LANES), lambda b, s: (0, 0))
    return pl.pallas_call(
        functools.partial(_qk_prep_kernel, nheads=nheads, head_dim=head_dim,
                          q_scale=head_dim ** -0.5 * LOG2_E),
        grid=(B, S // ts),
        in_specs=[pl.BlockSpec((1, ts, attn_w), lambda b, s: (b, s, qb)),
                  pl.BlockSpec((1, ts, attn_w), lambda b, s: (b, s, kb)),
                  tab_spec, tab_spec, row_spec, row_spec,
                  pl.BlockSpec((LANES, LANES), lambda b, s: (0, 0))],
        out_specs=[pl.BlockSpec((1, ts, attn_w), lambda b, s: (b, s, 0))] * 2,
        out_shape=[out, out],
        compiler_params=_params("parallel", "parallel"),
        name="qk_prep",
    )(z, z, cos_t, sin_t, jnp.tile(q_norm_g, 2).reshape(1, LANES),
      jnp.tile(k_norm_g, 2).reshape(1, LANES), pmat)


def _attn_kernel(q_ref, k_ref, v_ref, lams_ref, g_ref, o_ref, qq_sc, m_sc, acc_sc,
                 *, tq, tk, rc, head_dim, lam_init):
    qi = pl.program_id(2)
    q = q_ref[0]
    lane = lax.broadcasted_iota(jnp.int32, q.shape, 1)
    zero = jnp.zeros_like(q)
    qq_sc[0:tq] = jnp.where(lane < head_dim, q, zero)
    qq_sc[tq:2 * tq] = jnp.where(lane >= head_dim, q, zero)
    m_sc[...] = jnp.full_like(m_sc, NEG_INF)
    acc_sc[...] = jnp.zeros_like(acc_sc)
    ones = jnp.ones((tk, LANES), BF16)

    def block(ki, diagonal):
        start = pl.multiple_of(ki * tk, tk)
        k = k_ref[0, pl.ds(start, tk), :]
        v1 = jnp.concatenate([v_ref[0, pl.ds(start, tk), :], ones], axis=1)
        for c in range(2 * tq // rc):
            rows = slice(c * rc, (c + 1) * rc)
            ncol = (c * rc) % tq + rc if diagonal else tk
            s = _dot_nt(qq_sc[rows], k[:ncol])
            if diagonal:
                row = lax.broadcasted_iota(jnp.int32, s.shape, 0) + (c * rc) % tq
                col = lax.broadcasted_iota(jnp.int32, s.shape, 1)
                s = jnp.where(col <= row, s, NEG_INF)
            m_prev = m_sc[rows]
            m_new = jnp.maximum(m_prev, jnp.max(s, axis=1, keepdims=True))
            alpha = jnp.exp2(m_prev - m_new)
            p = jnp.exp2(s - jnp.concatenate([m_new] * (ncol // LANES), axis=1)).astype(BF16)
            pv = jnp.dot(p, v1[:ncol], preferred_element_type=F32)
            acc_sc[rows] = jnp.concatenate([alpha, alpha], axis=1) * acc_sc[rows] + pv
            m_sc[rows] = m_new

    def off_diagonal(ki, carry):
        block(ki, False)
        return carry

    lax.fori_loop(0, qi, off_diagonal, 0)
    block(qi, True)

    acc = acc_sc[...]
    o = acc[:, :LANES] / acc[:, LANES:]
    lv = lams_ref[...]
    lam = (jnp.exp(jnp.sum(lv[0:1] * lv[1:2], axis=1, keepdims=True))
           - jnp.exp(jnp.sum(lv[2:3] * lv[3:4], axis=1, keepdims=True)) + lam_init)
    d = o[0:tq] - lam * o[tq:2 * tq]
    ms = jnp.mean(d * d, axis=-1, keepdims=True)
    y = d * lax.rsqrt(ms + EPS) * g_ref[...] * (1.0 - lam_init)
    o_ref[0] = y.astype(o_ref.dtype)


def _attention(qn, kn, z, lams, subln_g, v_col, head_dim, lam_init):
    B, S, attn_w = qn.shape
    nheads = attn_w // LANES
    tq = tk = _pick(S, ATTN_BLOCK)
    vb = v_col // LANES
    return pl.pallas_call(
        functools.partial(_attn_kernel, tq=tq, tk=tk, rc=_pick(tq, ATTN_ROW_CHUNK),
                          head_dim=head_dim, lam_init=lam_init),
        grid=(B, nheads, S // tq),
        in_specs=[pl.BlockSpec((1, tq, LANES), lambda b, h, i: (b, i, h)),
                  pl.BlockSpec((1, S, LANES), lambda b, h, i: (b, 0, h)),
                  pl.BlockSpec((1, S, LANES), lambda b, h, i: (b, 0, vb + h)),
                  pl.BlockSpec((4, head_dim), lambda b, h, i: (0, 0)),
                  pl.BlockSpec((1, LANES), lambda b, h, i: (0, 0))],
        out_specs=pl.BlockSpec((1, tq, LANES), lambda b, h, i: (b, i, h)),
        out_shape=jax.ShapeDtypeStruct((B, S, attn_w), BF16),
        scratch_shapes=[pltpu.VMEM((2 * tq, LANES), BF16), pltpu.VMEM((2 * tq, LANES), F32),
                        pltpu.VMEM((2 * tq, 2 * LANES), F32)],
        compiler_params=_params("parallel", "parallel", "parallel"),
        name="diff_attention",
    )(qn, kn, z, lams, subln_g.reshape(1, LANES))


def _merge_kernel(rec_ref, att_ref, wl_ref, wa_ref, gl_ref, ga_ref, o_ref):
    yl = jnp.dot(rec_ref[...], wl_ref[...], preferred_element_type=F32)
    ya = jnp.dot(att_ref[...], wa_ref[...], preferred_element_type=F32)
    gl = jax.nn.sigmoid(gl_ref[...].astype(F32))
    ga = jax.nn.sigmoid(ga_ref[...].astype(F32))
    o_ref[...] = (gl * yl + ga * ya).astype(o_ref.dtype)


def _merge(rec, att, w_lru_out, w_attn_out, z2, gl_col, ga_col):
    T, lru_w = rec.shape
    attn_w = att.shape[1]
    D = w_lru_out.shape[1]
    tm, tn = _pick(T, 1024), _pick(math.gcd(D, gl_col, ga_col), 512)
    glb, gab = gl_col // tn, ga_col // tn
    return pl.pallas_call(
        _merge_kernel,
        grid=(T // tm, D // tn),
        in_specs=[pl.BlockSpec((tm, lru_w), lambda i, j: (i, 0)),
                  pl.BlockSpec((tm, attn_w), lambda i, j: (i, 0)),
                  pl.BlockSpec((lru_w, tn), lambda i, j: (0, j)),
                  pl.BlockSpec((attn_w, tn), lambda i, j: (0, j)),
                  pl.BlockSpec((tm, tn), lambda i, j: (i, glb + j)),
                  pl.BlockSpec((tm, tn), lambda i, j: (i, gab + j))],
        out_specs=pl.BlockSpec((tm, tn), lambda i, j: (i, j)),
        out_shape=jax.ShapeDtypeStruct((T, D), BF16),
        compiler_params=_params("parallel", "parallel"),
        name="merge",
    )(rec, att, w_lru_out, w_attn_out, z2, z2)


def _proj_res_kernel(a_ref, w_ref, x_ref, g_ref, o_ref):
    y = jnp.dot(a_ref[...], w_ref[...], preferred_element_type=F32)
    o_ref[...] = x_ref[...] + g_ref[0] * y


def _proj_residual(a, w, x2, gate, seq):
    T, K = a.shape
    D = w.shape[1]
    tm, tn = _pick(math.gcd(T, seq), 1024), _pick(D, 512)
    per_b = seq // tm
    return pl.pallas_call(
        _proj_res_kernel,
        grid=(T // tm, D // tn),
        in_specs=[pl.BlockSpec((tm, K), lambda i, j: (i, 0)),
                  pl.BlockSpec((K, tn), lambda i, j: (0, j)),
                  pl.BlockSpec((tm, tn), lambda i, j: (i, j)),
                  pl.BlockSpec((1, 1, tn), lambda i, j: (i // per_b, 0, j))],
        out_specs=pl.BlockSpec((tm, tn), lambda i, j: (i, j)),
        out_shape=jax.ShapeDtypeStruct((T, D), F32),
        compiler_params=_params("parallel", "parallel"),
        name="proj_residual",
    )(a, w, x2, gate)


def _topk_rows(s, extras, k):
    nrows = s.shape[0]
    rowf = lax.broadcasted_iota(jnp.int32, s.shape, 0).astype(F32)
    vals, picked = [], [[] for _ in extras]
    idxs = []
    for _ in range(k):
        m = jnp.max(s, axis=0, keepdims=True)
        idx = jnp.min(jnp.where(s == m, rowf, float(nrows)), axis=0, keepdims=True)
        sel = rowf == idx
        vals.append(m)
        idxs.append(idx)
        for lst, arr in zip(picked, extras):
            lst.append(jnp.max(jnp.where(sel, arr, -1.0), axis=0, keepdims=True))
        s = jnp.where(sel, -jnp.inf, s)
    return vals, idxs, picked


def _stack_rows(rows):
    k, n = len(rows), rows[0].shape[1]
    row = lax.broadcasted_iota(jnp.int32, (k, n), 0)
    out = jnp.zeros((k, n), F32)
    for r, v in enumerate(rows):
        out = jnp.where(row == r, v, out)
    return out


def _topk_kernel(q_ref, keys_ref, ia_ref, ib_ref, g_ref, *, k):
    q = q_ref[...]
    half = q.shape[1] // 2
    s1 = _dot_nt(keys_ref[0], q[:, :half], HIGHEST)
    s2 = _dot_nt(keys_ref[1], q[:, half:], HIGHEST)
    v1, i1, _ = _topk_rows(s1, (), k)
    v2, i2, _ = _topk_rows(s2, (), k)
    v1s, i1s = _stack_rows(v1), _stack_rows(i1)
    v2s, i2s = _stack_rows(v2), _stack_rows(i2)
    n = q.shape[0]
    vals, cas, cbs = [], [], []
    for i in range(k // 2):
        cnt = k // (i + 1)
        rows = -(-cnt // SUBLANES) * SUBLANES
        grp = v1[i] + v2s[0:rows]
        if cnt < rows:
            jrow = lax.broadcasted_iota(jnp.int32, grp.shape, 0)
            grp = jnp.where(jrow < cnt, grp, -jnp.inf)
        vals.append(grp)
        cas.append(jnp.broadcast_to(i1[i], (rows, n)))
        cbs.append(i2s[0:rows])
    vals.append(v1s[k // 2:k] + v2[0])
    cas.append(i1s[k // 2:k])
    cbs.append(jnp.broadcast_to(i2[0], (k - k // 2, n)))
    cand = jnp.concatenate(vals, axis=0)
    ca = jnp.concatenate(cas, axis=0)
    cb = jnp.concatenate(cbs, axis=0)
    best, _, (a_sel, b_sel) = _topk_rows(cand, (ca, cb), k)
    bests = _stack_rows(best)
    e = jnp.exp(bests - best[0])
    g_ref[...] = e / jnp.sum(e, axis=0, keepdims=True)
    ia_ref[...] = _stack_rows(a_sel)
    ib_ref[...] = _stack_rows(b_sel)


def _peer_topk(q, sub_keys, nheads):
    T = q.shape[0]
    nkeys, half = sub_keys.shape[1], sub_keys.shape[2]
    assert half == LANES and nkeys == LANES
    k = PEER_TOPK
    assert k % (2 * SUBLANES) == 0
    tt = _pick(T, 256)
    out = jax.ShapeDtypeStruct((nheads * k, T), F32)
    out_spec = pl.BlockSpec((k, tt), lambda i, h: (h, i))
    return pl.pallas_call(
        functools.partial(_topk_kernel, k=k),
        grid=(T // tt, nheads),
        in_specs=[pl.BlockSpec((tt, 2 * half), lambda i, h: (i, h)),
                  pl.BlockSpec((2, nkeys, half), lambda i, h: (0, 0, 0))],
        out_specs=[out_spec] * 3,
        out_shape=[out] * 3,
        compiler_params=_params("parallel", "parallel"),
        name="peer_topk",
    )(q, sub_keys)


def _gates_kernel(ia_ref, ib_ref, g_ref, o_ref, a_sc, b_sc, w_sc):
    a_sc[...] = ia_ref[...].T
    b_sc[...] = ib_ref[...].T
    w_sc[...] = g_ref[...].T
    nslots = ia_ref.shape[0]
    nkeys = o_ref.shape[1]
    keyf = lax.broadcasted_iota(jnp.int32, (nkeys, nslots), 0).astype(F32)

    def body(t, carry):
        a_row = a_sc[pl.ds(t, 1), :]
        b_row = b_sc[pl.ds(t, 1), :]
        w_row = w_sc[pl.ds(t, 1), :]
        lhs = jnp.where(keyf == a_row, w_row, 0.0).astype(BF16)
        rhs = jnp.where(keyf == b_row, 1.0, 0.0).astype(BF16)
        o_ref[t] = _dot_nt(lhs, rhs).astype(o_ref.dtype)
        return carry

    lax.fori_loop(0, o_ref.shape[0], body, 0)


def _peer_gates(ia, ib, g, nkeys):
    nslots, T = ia.shape
    tg = _pick(T, 128)
    in_spec = pl.BlockSpec((nslots, tg), lambda i: (0, i))
    return pl.pallas_call(
        _gates_kernel,
        grid=(T // tg,),
        in_specs=[in_spec] * 3,
        out_specs=pl.BlockSpec((tg, nkeys, nkeys), lambda i: (i, 0, 0)),
        out_shape=jax.ShapeDtypeStruct((T, nkeys, nkeys), BF16),
        scratch_shapes=[pltpu.VMEM((tg, nslots), F32)] * 3,
        compiler_params=_params("parallel"),
        name="peer_gates",
    )(ia, ib, g)


def _peer_kernel(h_ref, u_ref, v_ref, g_ref, o_ref):
    @pl.when(pl.program_id(1) == 0)
    def _():
        o_ref[...] = jnp.zeros_like(o_ref)

    s = _dot_nt(h_ref[...], u_ref[...])
    w = (_gelu(s) * g_ref[...].astype(F32)).astype(BF16)
    o_ref[...] += jnp.dot(w, v_ref[...], preferred_element_type=F32)


def _peer_dense(h2, u_tab, v_tab, gates):
    T, D = h2.shape
    E = u_tab.shape[0]
    tm, te = _pick(T, 512), _pick(E, 512)
    return pl.pallas_call(
        _peer_kernel,
        grid=(T // tm, E // te),
        in_specs=[pl.BlockSpec((tm, D), lambda i, j: (i, 0)),
                  pl.BlockSpec((te, D), lambda i, j: (j, 0)),
                  pl.BlockSpec((te, D), lambda i, j: (j, 0)),
                  pl.BlockSpec((tm, te), lambda i, j: (i, j))],
        out_specs=pl.BlockSpec((tm, D), lambda i, j: (i, 0)),
        out_shape=jax.ShapeDtypeStruct((T, D), F32),
        compiler_params=_params("parallel", "arbitrary"),
        name="peer_dense",
    )(h2, u_tab, v_tab, gates)


def _residual_kernel(x_ref, y_ref, g_ref, o_ref):
    o_ref[0] = x_ref[0] + g_ref[0] * y_ref[0]


def _residual(x, y, gate):
    B, S, D = x.shape
    ts = _pick(S, 256, SUBLANES)
    blk = pl.BlockSpec((1, ts, D), lambda b, s: (b, s, 0))
    return pl.pallas_call(
        _residual_kernel,
        grid=(B, S // ts),
        in_specs=[blk, blk, pl.BlockSpec((1, 1, D), lambda b, s: (b, 0, 0))],
        out_specs=blk,
        out_shape=jax.ShapeDtypeStruct((B, S, D), F32),
        compiler_params=_params("parallel", "parallel"),
        name="residual",
    )(x, y, gate)


def kernel(x, c, positions, norm1_g, norm2_g, w_ada, b_ada, w_in, conv_w, conv_b, w_a, b_a, w_x,
           b_x, lru_lambda, q_norm_g, k_norm_g, lambda_q1, lambda_k1, lambda_q2, lambda_k2,
           subln_g, w_lru_out, w_attn_out, w_o, w_pq, sub_keys, u_tab, v_tab):
    B, S, D = x.shape
    T = B * S
    depth = w_in.shape[0]
    lru_w = conv_w.shape[-1]
    attn_w = w_attn_out.shape[1]
    head_dim = q_norm_g.shape[-1]
    nkeys = sub_keys.shape[2]
    peer_heads = w_pq.shape[2] // (2 * sub_keys.shape[3])
    assert B <= SUBLANES and subln_g.shape[-1] == LANES
    q_col = 2 * lru_w
    k_col = q_col + attn_w
    v_col = k_col + attn_w
    gl_col = v_col + attn_w
    ga_col = gl_col + D

    c_pad = jnp.pad(c, ((0, SUBLANES - B), (0, 0)))
    mod = _ada(c_pad, w_ada, b_ada)[:, :B].reshape(depth, B, N_MOD, 1, D)
    cos_t, sin_t = _rope_tables(positions, head_dim)

    for l in range(depth):
        lam_init = 0.8 - 0.6 * math.exp(-0.3 * l)
        shift1, scale1, gate1 = mod[l, :, 0], mod[l, :, 1], mod[l, :, 2]
        shift2, scale2, gate2 = mod[l, :, 3], mod[l, :, 4], mod[l, :, 5]

        h = _norm_mod(x, norm1_g[l], scale1, shift1)
        z2 = _matmul(h.reshape(T, D), w_in[l].astype(BF16), BF16, "in_proj")
        z = z2.reshape(B, S, -1)
        rec = _lru(z, conv_w[l], conv_b[l], w_a[l], b_a[l], w_x[l], b_x[l], lru_lambda[l], lru_w)
        qn, kn = _qk_prep(z, cos_t, sin_t, q_norm_g[l], k_norm_g[l], q_col, k_col, attn_w, head_dim)
        lams = jnp.stack([lambda_q1[l], lambda_k1[l], lambda_q2[l], lambda_k2[l]])
        att = _attention(qn, kn, z, lams, subln_g[l], v_col, head_dim, lam_init)
        merged = _merge(rec.reshape(T, lru_w), att.reshape(T, attn_w), w_lru_out[l].astype(BF16),
                        w_attn_out[l].astype(BF16), z2, gl_col, ga_col)
        x = _proj_residual(merged, w_o[l].astype(BF16), x.reshape(T, D), gate1, S).reshape(B, S, D)

        h2 = _norm_mod(x, norm2_g[l], scale2, shift2).reshape(T, D)
        pq = _matmul(h2, w_pq[l].astype(BF16), F32, "peer_query")
        ia, ib, g = _peer_topk(pq, sub_keys[l], peer_heads)
        gates = _peer_gates(ia, ib, g, nkeys).reshape(T, nkeys * nkeys)
        y = _peer_dense(h2, u_tab[l].astype(BF16), v_tab[l].astype(BF16), gates)
        x = _residual(x, y.reshape(B, S, D), gate2)
    return x
```

```python
import functools
import math

import jax
import jax.numpy as jnp
from jax import lax
from jax.experimental import pallas as pl
from jax.experimental.pallas import tpu as pltpu

F32 = jnp.float32
BF16 = jnp.bfloat16

LANES = 128
SUBLANES = 8
VMEM_LIMIT = 56 * 1024 * 1024
EPS = 1e-6
NEG_INF = -1e30
LRU_C = 8.0
ROPE_THETA = 10000.0
PEER_TOPK = 16
N_MOD = 6
INV_SQRT2 = 0.7071067811865476
LOG2_E = 1.4426950408889634
ATTN_BLOCK = 1024
ATTN_ROW_CHUNK = 256
HIGHEST = lax.Precision.HIGHEST


def _pick(n, pref, align=LANES):
    if n <= pref:
        return n
    t = (pref // align) * align
    while t > align and n % t:
        t -= align
    assert n % t == 0, (n, pref, align)
    return t


def _params(*sem):
    return pltpu.CompilerParams(dimension_semantics=sem, vmem_limit_bytes=VMEM_LIMIT)


def _dot_nt(a, b, precision=None):
    return lax.dot_general(a, b, (((1,), (1,)), ((), ())), precision=precision,
                           preferred_element_type=F32)


def _gelu(x):
    return 0.5 * x * (1.0 + lax.erf(x * INV_SQRT2))


def _ada_kernel(c_ref, w_ref, b_ref, o_ref):
    c = c_ref[...]
    ca = c * jax.nn.sigmoid(c)
    o_ref[0] = jnp.dot(ca, w_ref[0], preferred_element_type=F32, precision=HIGHEST) + b_ref[0]


def _ada(c_pad, w_ada, b_ada):
    L, D, N = w_ada.shape
    tn = _pick(N, 512)
    return pl.pallas_call(
        _ada_kernel,
        grid=(L, N // tn),
        in_specs=[pl.BlockSpec((SUBLANES, D), lambda l, j: (0, 0)),
                  pl.BlockSpec((1, D, tn), lambda l, j: (l, 0, j)),
                  pl.BlockSpec((1, 1, tn), lambda l, j: (l, 0, j))],
        out_specs=pl.BlockSpec((1, SUBLANES, tn), lambda l, j: (l, 0, j)),
        out_shape=jax.ShapeDtypeStruct((L, SUBLANES, N), F32),
        compiler_params=_params("parallel", "parallel"),
        name="adaln",
    )(c_pad, w_ada, b_ada.reshape(L, 1, N))


def _norm_mod_kernel(x_ref, g_ref, sc_ref, sh_ref, o_ref):
    x = x_ref[0]
    ms = jnp.mean(x * x, axis=-1, keepdims=True)
    y = x * lax.rsqrt(ms + EPS) * g_ref[...]
    o_ref[0] = (y * (1.0 + sc_ref[0]) + sh_ref[0]).astype(o_ref.dtype)


def _norm_mod(x, g, scale, shift):
    B, S, D = x.shape
    ts = _pick(S, 256)
    return pl.pallas_call(
        _norm_mod_kernel,
        grid=(B, S // ts),
        in_specs=[pl.BlockSpec((1, ts, D), lambda b, s: (b, s, 0)),
                  pl.BlockSpec((1, D), lambda b, s: (0, 0)),
                  pl.BlockSpec((1, 1, D), lambda b, s: (b, 0, 0)),
                  pl.BlockSpec((1, 1, D), lambda b, s: (b, 0, 0))],
        out_specs=pl.BlockSpec((1, ts, D), lambda b, s: (b, s, 0)),
        out_shape=jax.ShapeDtypeStruct((B, S, D), BF16),
        compiler_params=_params("parallel", "parallel"),
        name="norm_mod",
    )(x, g.reshape(1, D), scale, shift)


def _mm_kernel(a_ref, b_ref, o_ref):
    o_ref[...] = jnp.dot(a_ref[...], b_ref[...], preferred_element_type=F32).astype(o_ref.dtype)


def _matmul(a, b, out_dtype, name):
    M, K = a.shape
    _, N = b.shape
    tm, tn = _pick(M, 1024), _pick(N, 512)
    return pl.pallas_call(
        _mm_kernel,
        grid=(M // tm, N // tn),
        in_specs=[pl.BlockSpec((tm, K), lambda i, j: (i, 0)),
                  pl.BlockSpec((K, tn), lambda i, j: (0, j))],
        out_specs=pl.BlockSpec((tm, tn), lambda i, j: (i, j)),
        out_shape=jax.ShapeDtypeStruct((M, N), out_dtype),
        compiler_params=_params("parallel", "parallel"),
        name=name,
    )(a, b)


def _shift_rows(x, s, fill):
    row = lax.broadcasted_iota(jnp.int32, x.shape, 0)
    return jnp.where(row >= s, pltpu.roll(x, s, 0), fill)


def _linear_scan(a, u):
    n = a.shape[0]
    s = 1
    while s < n:
        u = a * _shift_rows(u, s, 0.0) + u
        a = a * _shift_rows(a, s, 1.0)
        s *= 2
    return a, u


def _lru_kernel(x_ref, gate_ref, cw_ref, cb_ref, wa_ref, ba_ref, wx_ref, bx_ref, lam_ref,
                o_ref, tail_sc, h_sc, *, nblk):
    @pl.when(pl.program_id(2) == 0)
    def _():
        tail_sc[...] = jnp.zeros_like(tail_sc)
        h_sc[...] = jnp.zeros_like(h_sc)

    ts = x_ref.shape[1]
    row8 = lax.broadcasted_iota(jnp.int32, (SUBLANES, LANES), 0)
    for j in range(nblk):
        sl = slice(j * LANES, (j + 1) * LANES)
        x = x_ref[0, :, sl].astype(F32)
        tail = tail_sc[:, sl]
        cw = cw_ref[:, sl]
        nk = cw.shape[0]
        xc = cb_ref[:, sl] + cw[nk - 1:nk] * x
        for d in range(1, nk):
            xr = pltpu.roll(x, d, 0)
            head = jnp.where(row8 < d, pltpu.roll(tail, d, 0), xr[:SUBLANES])
            xd = jnp.concatenate([head, xr[SUBLANES:]], axis=0)
            xc = xc + cw[nk - 1 - d:nk - d] * xd
        tail_sc[:, sl] = x[ts - SUBLANES:]

        xcb = xc.astype(BF16)
        r = jax.nn.sigmoid(jnp.dot(xcb, wa_ref[j], preferred_element_type=F32) + ba_ref[:, sl])
        i = jax.nn.sigmoid(jnp.dot(xcb, wx_ref[j], preferred_element_type=F32) + bx_ref[:, sl])
        neg_lam = -lam_ref[:, sl]
        softplus = jnp.maximum(neg_lam, 0.0) + jnp.log1p(jnp.exp(-jnp.abs(neg_lam)))
        log_a = (-LRU_C) * r * softplus
        a = jnp.exp(log_a)
        u = jnp.sqrt(1.0 - a * a) * (i * xc)
        acum, h = _linear_scan(a, u)
        h = h + acum * h_sc[:, sl]
        h_sc[:, sl] = h[ts - 1:ts]
        gate = gate_ref[0, :, sl].astype(F32)
        o_ref[0, :, sl] = (h * _gelu(gate)).astype(o_ref.dtype)


def _lru(z, conv_w, conv_b, w_a, b_a, w_x, b_x, lru_lambda, lru_w):
    B, S, _ = z.shape
    nblocks = w_a.shape[0]
    assert w_a.shape[1] == LANES and lru_w == nblocks * LANES
    cw = _pick(lru_w, 512)
    nblk = cw // LANES
    ts = _pick(S, 512, SUBLANES)
    ncb = lru_w // cw
    row = lambda v: v.reshape(1, lru_w)
    vec_spec = pl.BlockSpec((1, cw), lambda b, c, s: (0, c))
    return pl.pallas_call(
        functools.partial(_lru_kernel, nblk=nblk),
        grid=(B, ncb, S // ts),
        in_specs=[pl.BlockSpec((1, ts, cw), lambda b, c, s: (b, s, c)),
                  pl.BlockSpec((1, ts, cw), lambda b, c, s: (b, s, c + ncb)),
                  pl.BlockSpec((conv_w.shape[0], cw), lambda b, c, s: (0, c)),
                  vec_spec,
                  pl.BlockSpec((nblk, LANES, LANES), lambda b, c, s: (c, 0, 0)),
                  vec_spec,
                  pl.BlockSpec((nblk, LANES, LANES), lambda b, c, s: (c, 0, 0)),
                  vec_spec, vec_spec],
        out_specs=pl.BlockSpec((1, ts, cw), lambda b, c, s: (b, s, c)),
        out_shape=jax.ShapeDtypeStruct((B, S, lru_w), BF16),
        scratch_shapes=[pltpu.VMEM((SUBLANES, cw), F32), pltpu.VMEM((1, cw), F32)],
        compiler_params=_params("parallel", "parallel", "arbitrary"),
        name="rg_lru",
    )(z, z, conv_w, row(conv_b), w_a.astype(BF16), row(b_a), w_x.astype(BF16), row(b_x),
      row(lru_lambda))


def _rope_kernel(pos_ref, f_ref, sg_ref, cos_ref, sin_ref):
    ang = pos_ref[0].astype(F32) * f_ref[...]
    cos_ref[0] = jnp.cos(ang)
    sin_ref[0] = jnp.sin(ang) * sg_ref[...]


def _rope_tables(positions, head_dim):
    B, S = positions.shape
    half = head_dim // 2
    inv_freq = ROPE_THETA ** (-jnp.arange(0, head_dim, 2, dtype=F32) / head_dim)
    freq_row = jnp.tile(inv_freq, LANES // half).reshape(1, LANES)
    sign_row = jnp.tile(jnp.concatenate([-jnp.ones(half, F32), jnp.ones(half, F32)]),
                        LANES // head_dim).reshape(1, LANES)
    ts = _pick(S, 1024, SUBLANES)
    tab = jax.ShapeDtypeStruct((B, S, LANES), F32)
    return pl.pallas_call(
        _rope_kernel,
        grid=(B, S // ts),
        in_specs=[pl.BlockSpec((1, ts, 1), lambda b, s: (b, s, 0)),
                  pl.BlockSpec((1, LANES), lambda b, s: (0, 0)),
                  pl.BlockSpec((1, LANES), lambda b, s: (0, 0))],
        out_specs=[pl.BlockSpec((1, ts, LANES), lambda b, s: (b, s, 0))] * 2,
        out_shape=[tab, tab],
        compiler_params=_params("parallel", "parallel"),
        name="rope_tables",
    )(positions.reshape(B, S, 1), freq_row, sign_row)


def _qk_prep_kernel(q_ref, k_ref, cos_ref, sin_ref, gq_ref, gk_ref, p_ref, qo_ref, ko_ref,
                    *, nheads, head_dim, q_scale):
    cos = cos_ref[0]
    sin = sin_ref[0]
    pmat = p_ref[...]
    lane = lax.broadcasted_iota(jnp.int32, cos.shape, 1)
    first_half = (lane % head_dim) < (head_dim // 2)

    def prep(t, g):
        ms = jnp.dot(t * t, pmat, preferred_element_type=F32, precision=HIGHEST)
        tn = t * lax.rsqrt(ms + EPS) * g
        swapped = jnp.where(first_half, pltpu.roll(tn, LANES - head_dim // 2, 1),
                            pltpu.roll(tn, head_dim // 2, 1))
        return tn * cos + swapped * sin

    for h in range(nheads):
        sl = slice(h * LANES, (h + 1) * LANES)
        qo_ref[0, :, sl] = (prep(q_ref[0, :, sl].astype(F32), gq_ref[...]) * q_scale).astype(BF16)
        ko_ref[0, :, sl] = prep(k_ref[0, :, sl].astype(F32), gk_ref[...]).astype(BF16)


def _qk_prep(z, cos_t, sin_t, q_norm_g, k_norm_g, q_col, k_col, attn_w, head_dim):
    B, S, _ = z.shape
    assert 2 * head_dim == LANES
    nheads = attn_w // LANES
    ts = _pick(S, 256, SUBLANES)
    qb, kb = q_col // attn_w, k_col // attn_w
    lane = jnp.arange(LANES)
    pmat = (lane[:, None] // head_dim == lane[None, :] // head_dim).astype(F32) / head_dim
    out = jax.ShapeDtypeStruct((B, S, attn_w), BF16)
    tab_spec = pl.BlockSpec((1, ts, LANES), lambda b, s: (b, s, 0))
    row_spec = pl.BlockSpec((1, LANES), lambda b, s: (0, 0))
    return pl.pallas_call(
        functools.partial(_qk_prep_kernel, nheads=nheads, head_dim=head_dim,
                          q_scale=head_dim ** -0.5 * LOG2_E),
        grid=(B, S // ts),
        in_specs=[pl.BlockSpec((1, ts, attn_w), lambda b, s: (b, s, qb)),
                  pl.BlockSpec((1, ts, attn_w), lambda b, s: (b, s, kb)),
                  tab_spec, tab_spec, row_spec, row_spec,
                  pl.BlockSpec((LANES, LANES), lambda b, s: (0, 0))],
        out_specs=[pl.BlockSpec((1, ts, attn_w), lambda b, s: (b, s, 0))] * 2,
        out_shape=[out, out],
        compiler_params=_params("parallel", "parallel"),
        name="qk_prep",
    )(z, z, cos_t, sin_t, jnp.tile(q_norm_g, 2).reshape(1, LANES),
      jnp.tile(k_norm_g, 2).reshape(1, LANES), pmat)


def _attn_kernel(q_ref, k_ref, v_ref, lams_ref, g_ref, o_ref, qq_sc, m_sc, acc_sc,
                 *, tq, tk, rc, head_dim, lam_init):
    qi = pl.program_id(2)
    q = q_ref[0]
    lane = lax.broadcasted_iota(jnp.int32, q.shape, 1)
    zero = jnp.zeros_like(q)
    qq_sc[0:tq] = jnp.where(lane < head_dim, q, zero)
    qq_sc[tq:2 * tq] = jnp.where(lane >= head_dim, q, zero)
    m_sc[...] = jnp.full_like(m_sc, NEG_INF)
    acc_sc[...] = jnp.zeros_like(acc_sc)
    ones = jnp.ones((tk, LANES), BF16)

    def block(ki, diagonal):
        start = pl.multiple_of(ki * tk, tk)
        k = k_ref[0, pl.ds(start, tk), :]
        v1 = jnp.concatenate([v_ref[0, pl.ds(start, tk), :], ones], axis=1)
        for c in range(2 * tq // rc):
            rows = slice(c * rc, (c + 1) * rc)
            ncol = (c * rc) % tq + rc if diagonal else tk
            s = _dot_nt(qq_sc[rows], k[:ncol])
            if diagonal:
                row = lax.broadcasted_iota(jnp.int32, s.shape, 0) + (c * rc) % tq
                col = lax.broadcasted_iota(jnp.int32, s.shape, 1)
                s = jnp.where(col <= row, s, NEG_INF)
            m_prev = m_sc[rows]
            m_new = jnp.maximum(m_prev, jnp.max(s, axis=1, keepdims=True))
            alpha = jnp.exp2(m_prev - m_new)
            p = jnp.exp2(s - jnp.concatenate([m_new] * (ncol // LANES), axis=1)).astype(BF16)
            pv = jnp.dot(p, v1[:ncol], preferred_element_type=F32)
            acc_sc[rows] = jnp.concatenate([alpha, alpha], axis=1) * acc_sc[rows] + pv
            m_sc[rows] = m_new

    def off_diagonal(ki, carry):
        block(ki, False)
        return carry

    lax.fori_loop(0, qi, off_diagonal, 0)
    block(qi, True)

    acc = acc_sc[...]
    o = acc[:, :LANES] / acc[:, LANES:]
    lv = lams_ref[...]
    lam = (jnp.exp(jnp.sum(lv[0:1] * lv[1:2], axis=1, keepdims=True))
           - jnp.exp(jnp.sum(lv[2:3] * lv[3:4], axis=1, keepdims=True)) + lam_init)
    d = o[0:tq] - lam * o[tq:2 * tq]
    ms = jnp.mean(d * d, axis=-1, keepdims=True)
    y = d * lax.rsqrt(ms + EPS) * g_ref[...] * (1.0 - lam_init)
    o_ref[0] = y.astype(o_ref.dtype)


def _attention(qn, kn, z, lams, subln_g, v_col, head_dim, lam_init):
    B, S, attn_w = qn.shape
    nheads = attn_w // LANES
    tq = tk = _pick(S, ATTN_BLOCK)
    vb = v_col // LANES
    return pl.pallas_call(
        functools.partial(_attn_kernel, tq=tq, tk=tk, rc=_pick(tq, ATTN_ROW_CHUNK),
                          head_dim=head_dim, lam_init=lam_init),
        grid=(B, nheads, S // tq),
        in_specs=[pl.BlockSpec((1, tq, LANES), lambda b, h, i: (b, i, h)),
                  pl.BlockSpec((1, S, LANES), lambda b, h, i: (b, 0, h)),
                  pl.BlockSpec((1, S, LANES), lambda b, h, i: (b, 0, vb + h)),
                  pl.BlockSpec((4, head_dim), lambda b, h, i: (0, 0)),
                  pl.BlockSpec((1, LANES), lambda b, h, i: (0, 0))],
        out_specs=pl.BlockSpec((1, tq, LANES), lambda b, h, i: (b, i, h)),
        out_shape=jax.ShapeDtypeStruct((B, S, attn_w), BF16),
        scratch_shapes=[pltpu.VMEM((2 * tq, LANES), BF16), pltpu.VMEM((2 * tq, LANES), F32),
                        pltpu.VMEM((2 * tq, 2 * LANES), F32)],
        compiler_params=_params("parallel", "parallel", "parallel"),
        name="diff_attention",
    )(qn, kn, z, lams, subln_g.reshape(1, LANES))


def _merge_kernel(rec_ref, att_ref, wl_ref, wa_ref, gl_ref, ga_ref, o_ref):
    yl = jnp.dot(rec_ref[...], wl_ref[...], preferred_element_type=F32)
    ya = jnp.dot(att_ref[...], wa_ref[...], preferred_element_type=F32)
    gl = jax.nn.sigmoid(gl_ref[...].astype(F32))
    ga = jax.nn.sigmoid(ga_ref[...].astype(F32))
    o_ref[...] = (gl * yl + ga * ya).astype(o_ref.dtype)


def _merge(rec, att, w_lru_out, w_attn_out, z2, gl_col, ga_col):
    T, lru_w = rec.shape
    attn_w = att.shape[1]
    D = w_lru_out.shape[1]
    tm, tn = _pick(T, 1024), _pick(math.gcd(D, gl_col, ga_col), 512)
    glb, gab = gl_col // tn, ga_col // tn
    return pl.pallas_call(
        _merge_kernel,
        grid=(T // tm, D // tn),
        in_specs=[pl.BlockSpec((tm, lru_w), lambda i, j: (i, 0)),
                  pl.BlockSpec((tm, attn_w), lambda i, j: (i, 0)),
                  pl.BlockSpec((lru_w, tn), lambda i, j: (0, j)),
                  pl.BlockSpec((attn_w, tn), lambda i, j: (0, j)),
                  pl.BlockSpec((tm, tn), lambda i, j: (i, glb + j)),
                  pl.BlockSpec((tm, tn), lambda i, j: (i, gab + j))],
        out_specs=pl.BlockSpec((tm, tn), lambda i, j: (i, j)),
        out_shape=jax.ShapeDtypeStruct((T, D), BF16),
        compiler_params=_params("parallel", "parallel"),
        name="merge",
    )(rec, att, w_lru_out, w_attn_out, z2, z2)


def _proj_res_kernel(a_ref, w_ref, x_ref, g_ref, o_ref):
    y = jnp.dot(a_ref[...], w_ref[...], preferred_element_type=F32)
    o_ref[...] = x_ref[...] + g_ref[0] * y


def _proj_residual(a, w, x2, gate, seq):
    T, K = a.shape
    D = w.shape[1]
    tm, tn = _pick(math.gcd(T, seq), 1024), _pick(D, 512)
    per_b = seq // tm
    return pl.pallas_call(
        _proj_res_kernel,
        grid=(T // tm, D // tn),
        in_specs=[pl.BlockSpec((tm, K), lambda i, j: (i, 0)),
                  pl.BlockSpec((K, tn), lambda i, j: (0, j)),
                  pl.BlockSpec((tm, tn), lambda i, j: (i, j)),
                  pl.BlockSpec((1, 1, tn), lambda i, j: (i // per_b, 0, j))],
        out_specs=pl.BlockSpec((tm, tn), lambda i, j: (i, j)),
        out_shape=jax.ShapeDtypeStruct((T, D), F32),
        compiler_params=_params("parallel", "parallel"),
        name="proj_residual",
    )(a, w, x2, gate)


def _topk_rows(s, extras, k):
    nrows = s.shape[0]
    rowf = lax.broadcasted_iota(jnp.int32, s.shape, 0).astype(F32)
    vals, picked = [], [[] for _ in extras]
    idxs = []
    rank = jnp.full(s.shape, float(k), F32)
    for r in range(k):
        m = jnp.max(s, axis=0, keepdims=True)
        idx = jnp.min(jnp.where(s == m, rowf, float(nrows)), axis=0, keepdims=True)
        sel = rowf == idx
        vals.append(m)
        idxs.append(idx)
        for lst, arr in zip(picked, extras):
            lst.append(jnp.max(jnp.where(sel, arr, -1.0), axis=0, keepdims=True))
        s = jnp.where(sel, -jnp.inf, s)
        rank = jnp.where(sel, float(r), rank)
    return vals, idxs, picked, rank


def _stack_rows(rows):
    k, n = len(rows), rows[0].shape[1]
    row = lax.broadcasted_iota(jnp.int32, (k, n), 0)
    out = jnp.zeros((k, n), F32)
    for r, v in enumerate(rows):
        out = jnp.where(row == r, v, out)
    return out


def _topk_kernel(q_ref, keys_ref, kap_ref, cnt_ref, e2_ref, rk2_ref, *, k):
    q = q_ref[...]
    half = q.shape[1] // 2
    s1 = _dot_nt(keys_ref[0], q[:, :half], HIGHEST)
    s2 = _dot_nt(keys_ref[1], q[:, half:], HIGHEST)
    v1, i1, _, _ = _topk_rows(s1, (), k)
    v2, _, _, rk2 = _topk_rows(s2, (), k)
    v1s, i1s = _stack_rows(v1), _stack_rows(i1)
    v2s = _stack_rows(v2)
    n = q.shape[0]
    vals, cas = [], []
    for i in range(k // 2):
        cnt = k // (i + 1)
        rows = -(-cnt // SUBLANES) * SUBLANES
        grp = v1[i] + v2s[0:rows]
        if cnt < rows:
            jrow = lax.broadcasted_iota(jnp.int32, grp.shape, 0)
            grp = jnp.where(jrow < cnt, grp, -jnp.inf)
        vals.append(grp)
        cas.append(jnp.broadcast_to(i1[i], (rows, n)))
    vals.append(v1s[k // 2:k] + v2[0])
    cas.append(i1s[k // 2:k])
    best, _, (a_sel,), _ = _topk_rows(jnp.concatenate(vals, axis=0),
                                      (jnp.concatenate(cas, axis=0),), k)
    z = jnp.sum(jnp.exp(_stack_rows(best) - best[0]), axis=0, keepdims=True)
    keyf = lax.broadcasted_iota(jnp.int32, s1.shape, 0).astype(F32)
    count = jnp.zeros(s1.shape, F32)
    for a in a_sel:
        count = count + jnp.where(keyf == a, 1.0, 0.0)
    kap_ref[...] = jnp.exp(s1 - v1[0]) / z
    cnt_ref[...] = count
    e2_ref[...] = jnp.exp(s2 - v2[0])
    rk2_ref[...] = rk2


def _peer_topk(q, sub_keys, nheads):
    T = q.shape[0]
    nkeys, half = sub_keys.shape[1], sub_keys.shape[2]
    assert half == LANES and nkeys == LANES
    k = PEER_TOPK
    assert k % (2 * SUBLANES) == 0
    tt = _pick(T, 256)
    out = jax.ShapeDtypeStruct((nheads * nkeys, T), F32)
    out_spec = pl.BlockSpec((nkeys, tt), lambda i, h: (h, i))
    return pl.pallas_call(
        functools.partial(_topk_kernel, k=k),
        grid=(T // tt, nheads),
        in_specs=[pl.BlockSpec((tt, 2 * half), lambda i, h: (i, h)),
                  pl.BlockSpec((2, nkeys, half), lambda i, h: (0, 0, 0))],
        out_specs=[out_spec] * 4,
        out_shape=[out] * 4,
        compiler_params=_params("parallel", "parallel"),
        name="peer_topk",
    )(q, sub_keys)


def _peer_kernel(h_ref, u_ref, vt_ref, kap_ref, cnt_ref, e2_ref, rk2_ref, o_ref, w_sc,
                 *, nheads, nkeys):
    j = pl.program_id(1)
    nj = pl.num_programs(1) - 1
    ka = u_ref.shape[0] // nkeys

    def weights():
        parts = []
        for aa in range(ka):
            a = j * ka + aa
            g = None
            for h in range(nheads):
                kap = kap_ref[pl.ds(h * nkeys + a, 1), :]
                cnt = cnt_ref[pl.ds(h * nkeys + a, 1), :]
                e2 = e2_ref[h * nkeys:(h + 1) * nkeys, :]
                rk2 = rk2_ref[h * nkeys:(h + 1) * nkeys, :]
                term = jnp.where(rk2 < cnt, e2, 0.0) * kap
                g = term if g is None else g + term
            parts.append(g)
        gates = jnp.concatenate(parts, axis=0)
        s = _dot_nt(u_ref[...], h_ref[...])
        return (_gelu(s) * gates).astype(BF16)

    cur = j % 2

    @pl.when(j == 0)
    def _():
        o_ref[...] = jnp.zeros_like(o_ref)
        w_sc[0] = weights()

    @pl.when(jnp.logical_and(j > 0, j < nj))
    def _():
        w_next = weights()
        o_ref[...] += jnp.dot(vt_ref[...], w_sc[1 - cur], preferred_element_type=F32)
        w_sc[cur] = w_next

    @pl.when(j == nj)
    def _():
        o_ref[...] += jnp.dot(vt_ref[...], w_sc[1 - cur], preferred_element_type=F32)


def _peer_dense(h2, u_tab, vt_tab, kap, cnt, e2, rk2, nheads, nkeys):
    T, D = h2.shape
    E = u_tab.shape[0]
    tm, te = _pick(T, 512), _pick(E, 512)
    assert te % nkeys == 0
    once = pl.Buffered(1)
    tab_spec = pl.BlockSpec((nheads * nkeys, tm), lambda i, j: (0, i), pipeline_mode=once)
    nj = E // te
    return pl.pallas_call(
        functools.partial(_peer_kernel, nheads=nheads, nkeys=nkeys),
        grid=(T // tm, nj + 1),
        in_specs=[pl.BlockSpec((tm, D), lambda i, j: (i, 0), pipeline_mode=once),
                  pl.BlockSpec((te, D), lambda i, j: (jnp.minimum(j, nj - 1), 0)),
                  pl.BlockSpec((D, te), lambda i, j: (0, jnp.maximum(j - 1, 0))),
                  tab_spec, tab_spec, tab_spec, tab_spec],
        out_specs=pl.BlockSpec((D, tm), lambda i, j: (0, i)),
        out_shape=jax.ShapeDtypeStruct((D, T), F32),
        scratch_shapes=[pltpu.VMEM((2, te, tm), BF16)],
        compiler_params=_params("parallel", "arbitrary"),
        name="peer_dense",
    )(h2, u_tab, vt_tab, kap, cnt, e2, rk2)


def _residual_kernel(x_ref, yt_ref, g_ref, o_ref):
    o_ref[0] = x_ref[0] + g_ref[0] * yt_ref[...].T


def _residual(x, y_t, gate):
    B, S, D = x.shape
    ts = _pick(S, 256)
    per_b = S // ts
    blk = pl.BlockSpec((1, ts, D), lambda b, s: (b, s, 0))
    return pl.pallas_call(
        _residual_kernel,
        grid=(B, per_b),
        in_specs=[blk, pl.BlockSpec((D, ts), lambda b, s: (0, b * per_b + s)),
                  pl.BlockSpec((1, 1, D), lambda b, s: (b, 0, 0))],
        out_specs=blk,
        out_shape=jax.ShapeDtypeStruct((B, S, D), F32),
        compiler_params=_params("parallel", "parallel"),
        name="residual",
    )(x, y_t, gate)


def kernel(x, c, positions, norm1_g, norm2_g, w_ada, b_ada, w_in, conv_w, conv_b, w_a, b_a, w_x,
           b_x, lru_lambda, q_norm_g, k_norm_g, lambda_q1, lambda_k1, lambda_q2, lambda_k2,
           subln_g, w_lru_out, w_attn_out, w_o, w_pq, sub_keys, u_tab, v_tab):
    B, S, D = x.shape
    T = B * S
    depth = w_in.shape[0]
    lru_w = conv_w.shape[-1]
    attn_w = w_attn_out.shape[1]
    head_dim = q_norm_g.shape[-1]
    nkeys = sub_keys.shape[2]
    peer_heads = w_pq.shape[2] // (2 * sub_keys.shape[3])
    assert B <= SUBLANES and subln_g.shape[-1] == LANES
    q_col = 2 * lru_w
    k_col = q_col + attn_w
    v_col = k_col + attn_w
    gl_col = v_col + attn_w
    ga_col = gl_col + D

    c_pad = jnp.pad(c, ((0, SUBLANES - B), (0, 0)))
    mod = _ada(c_pad, w_ada, b_ada)[:, :B].reshape(depth, B, N_MOD, 1, D)
    cos_t, sin_t = _rope_tables(positions, head_dim)

    for l in range(depth):
        lam_init = 0.8 - 0.6 * math.exp(-0.3 * l)
        shift1, scale1, gate1 = mod[l, :, 0], mod[l, :, 1], mod[l, :, 2]
        shift2, scale2, gate2 = mod[l, :, 3], mod[l, :, 4], mod[l, :, 5]

        h = _norm_mod(x, norm1_g[l], scale1, shift1)
        z2 = _matmul(h.reshape(T, D), w_in[l].astype(BF16), BF16, "in_proj")
        z = z2.reshape(B, S, -1)
        rec = _lru(z, conv_w[l], conv_b[l], w_a[l], b_a[l], w_x[l], b_x[l], lru_lambda[l], lru_w)
        qn, kn = _qk_prep(z, cos_t, sin_t, q_norm_g[l], k_norm_g[l], q_col, k_col, attn_w, head_dim)
        lams = jnp.stack([lambda_q1[l], lambda_k1[l], lambda_q2[l], lambda_k2[l]])
        att = _attention(qn, kn, z, lams, subln_g[l], v_col, head_dim, lam_init)
        merged = _merge(rec.reshape(T, lru_w), att.reshape(T, attn_w), w_lru_out[l].astype(BF16),
                        w_attn_out[l].astype(BF16), z2, gl_col, ga_col)
        x = _proj_residual(merged, w_o[l].astype(BF16), x.reshape(T, D), gate1, S).reshape(B, S, D)

        h2 = _norm_mod(x, norm2_g[l], scale2, shift2).reshape(T, D)
        pq = _matmul(h2, w_pq[l].astype(BF16), F32, "peer_query")
        kap, cnt, e2, rk2 = _peer_topk(pq, sub_keys[l], peer_heads)
        y_t = _peer_dense(h2, u_tab[l].astype(BF16), v_tab[l].T.astype(BF16), kap, cnt, e2, rk2,
                          peer_heads, nkeys)
        x = _residual(x, y_t, gate2)
    return x
```

```python
import functools
import math

import jax
import jax.numpy as jnp
from jax import lax
from jax.experimental import pallas as pl
from jax.experimental.pallas import tpu as pltpu

F32 = jnp.float32
BF16 = jnp.bfloat16

LANES = 128
SUBLANES = 8
VMEM_LIMIT = 56 * 1024 * 1024
EPS = 1e-6
NEG_INF = -1e30
LRU_C = 8.0
ROPE_THETA = 10000.0
PEER_TOPK = 16
N_MOD = 6
INV_SQRT2 = 0.7071067811865476
LOG2_E = 1.4426950408889634
ATTN_BLOCK = 1024
ATTN_ROW_CHUNK = 256
HIGHEST = lax.Precision.HIGHEST


def _pick(n, pref, align=LANES):
    if n <= pref:
        return n
    t = (pref // align) * align
    while t > align and n % t:
        t -= align
    assert n % t == 0, (n, pref, align)
    return t


def _params(*sem):
    return pltpu.CompilerParams(dimension_semantics=sem, vmem_limit_bytes=VMEM_LIMIT)


def _dot_nt(a, b, precision=None):
    return lax.dot_general(a, b, (((1,), (1,)), ((), ())), precision=precision,
                           preferred_element_type=F32)


def _gelu(x):
    return 0.5 * x * (1.0 + lax.erf(x * INV_SQRT2))


def _ada_kernel(c_ref, w_ref, b_ref, o_ref):
    c = c_ref[...]
    ca = c * jax.nn.sigmoid(c)
    o_ref[0] = jnp.dot(ca, w_ref[0], preferred_element_type=F32, precision=HIGHEST) + b_ref[0]


def _ada(c_pad, w_ada, b_ada):
    L, D, N = w_ada.shape
    tn = _pick(N, 512)
    return pl.pallas_call(
        _ada_kernel,
        grid=(L, N // tn),
        in_specs=[pl.BlockSpec((SUBLANES, D), lambda l, j: (0, 0)),
                  pl.BlockSpec((1, D, tn), lambda l, j: (l, 0, j)),
                  pl.BlockSpec((1, 1, tn), lambda l, j: (l, 0, j))],
        out_specs=pl.BlockSpec((1, SUBLANES, tn), lambda l, j: (l, 0, j)),
        out_shape=jax.ShapeDtypeStruct((L, SUBLANES, N), F32),
        compiler_params=_params("parallel", "parallel"),
        name="adaln",
    )(c_pad, w_ada, b_ada.reshape(L, 1, N))


def _norm_mod_kernel(x_ref, g_ref, sc_ref, sh_ref, o_ref):
    x = x_ref[0]
    ms = jnp.mean(x * x, axis=-1, keepdims=True)
    y = x * lax.rsqrt(ms + EPS) * g_ref[...]
    o_ref[0] = (y * (1.0 + sc_ref[0]) + sh_ref[0]).astype(o_ref.dtype)


def _norm_mod(x, g, scale, shift):
    B, S, D = x.shape
    ts = _pick(S, 256)
    return pl.pallas_call(
        _norm_mod_kernel,
        grid=(B, S // ts),
        in_specs=[pl.BlockSpec((1, ts, D), lambda b, s: (b, s, 0)),
                  pl.BlockSpec((1, D), lambda b, s: (0, 0)),
                  pl.BlockSpec((1, 1, D), lambda b, s: (b, 0, 0)),
                  pl.BlockSpec((1, 1, D), lambda b, s: (b, 0, 0))],
        out_specs=pl.BlockSpec((1, ts, D), lambda b, s: (b, s, 0)),
        out_shape=jax.ShapeDtypeStruct((B, S, D), BF16),
        compiler_params=_params("parallel", "parallel"),
        name="norm_mod",
    )(x, g.reshape(1, D), scale, shift)


def _mm_kernel(a_ref, b_ref, o_ref):
    o_ref[...] = jnp.dot(a_ref[...], b_ref[...], preferred_element_type=F32).astype(o_ref.dtype)


def _matmul(a, b, out_dtype, name):
    M, K = a.shape
    _, N = b.shape
    tm, tn = _pick(M, 1024), _pick(N, 512)
    return pl.pallas_call(
        _mm_kernel,
        grid=(M // tm, N // tn),
        in_specs=[pl.BlockSpec((tm, K), lambda i, j: (i, 0)),
                  pl.BlockSpec((K, tn), lambda i, j: (0, j))],
        out_specs=pl.BlockSpec((tm, tn), lambda i, j: (i, j)),
        out_shape=jax.ShapeDtypeStruct((M, N), out_dtype),
        compiler_params=_params("parallel", "parallel"),
        name=name,
    )(a, b)


def _shift_rows(x, s, fill):
    row = lax.broadcasted_iota(jnp.int32, x.shape, 0)
    return jnp.where(row >= s, pltpu.roll(x, s, 0), fill)


def _linear_scan(a, u):
    n = a.shape[0]
    s = 1
    while s < n:
        u = a * _shift_rows(u, s, 0.0) + u
        a = a * _shift_rows(a, s, 1.0)
        s *= 2
    return a, u


def _lru_kernel(x_ref, gate_ref, cw_ref, cb_ref, wa_ref, ba_ref, wx_ref, bx_ref, lam_ref,
                o_ref, tail_sc, h_sc, *, nblk):
    @pl.when(pl.program_id(2) == 0)
    def _():
        tail_sc[...] = jnp.zeros_like(tail_sc)
        h_sc[...] = jnp.zeros_like(h_sc)

    ts = x_ref.shape[1]
    row8 = lax.broadcasted_iota(jnp.int32, (SUBLANES, LANES), 0)
    for j in range(nblk):
        sl = slice(j * LANES, (j + 1) * LANES)
        x = x_ref[0, :, sl].astype(F32)
        tail = tail_sc[:, sl]
        cw = cw_ref[:, sl]
        nk = cw.shape[0]
        xc = cb_ref[:, sl] + cw[nk - 1:nk] * x
        for d in range(1, nk):
            xr = pltpu.roll(x, d, 0)
            head = jnp.where(row8 < d, pltpu.roll(tail, d, 0), xr[:SUBLANES])
            xd = jnp.concatenate([head, xr[SUBLANES:]], axis=0)
            xc = xc + cw[nk - 1 - d:nk - d] * xd
        tail_sc[:, sl] = x[ts - SUBLANES:]

        xcb = xc.astype(BF16)
        r = jax.nn.sigmoid(jnp.dot(xcb, wa_ref[j], preferred_element_type=F32) + ba_ref[:, sl])
        i = jax.nn.sigmoid(jnp.dot(xcb, wx_ref[j], preferred_element_type=F32) + bx_ref[:, sl])
        neg_lam = -lam_ref[:, sl]
        softplus = jnp.maximum(neg_lam, 0.0) + jnp.log1p(jnp.exp(-jnp.abs(neg_lam)))
        log_a = (-LRU_C) * r * softplus
        a = jnp.exp(log_a)
        u = jnp.sqrt(1.0 - a * a) * (i * xc)
        acum, h = _linear_scan(a, u)
        h = h + acum * h_sc[:, sl]
        h_sc[:, sl] = h[ts - 1:ts]
        gate = gate_ref[0, :, sl].astype(F32)
        o_ref[0, :, sl] = (h * _gelu(gate)).astype(o_ref.dtype)


def _lru(z, conv_w, conv_b, w_a, b_a, w_x, b_x, lru_lambda, lru_w):
    B, S, _ = z.shape
    nblocks = w_a.shape[0]
    assert w_a.shape[1] == LANES and lru_w == nblocks * LANES
    cw = _pick(lru_w, 512)
    nblk = cw // LANES
    ts = _pick(S, 512, SUBLANES)
    ncb = lru_w // cw
    row = lambda v: v.reshape(1, lru_w)
    vec_spec = pl.BlockSpec((1, cw), lambda b, c, s: (0, c))
    return pl.pallas_call(
        functools.partial(_lru_kernel, nblk=nblk),
        grid=(B, ncb, S // ts),
        in_specs=[pl.BlockSpec((1, ts, cw), lambda b, c, s: (b, s, c)),
                  pl.BlockSpec((1, ts, cw), lambda b, c, s: (b, s, c + ncb)),
                  pl.BlockSpec((conv_w.shape[0], cw), lambda b, c, s: (0, c)),
                  vec_spec,
                  pl.BlockSpec((nblk, LANES, LANES), lambda b, c, s: (c, 0, 0)),
                  vec_spec,
                  pl.BlockSpec((nblk, LANES, LANES), lambda b, c, s: (c, 0, 0)),
                  vec_spec, vec_spec],
        out_specs=pl.BlockSpec((1, ts, cw), lambda b, c, s: (b, s, c)),
        out_shape=jax.ShapeDtypeStruct((B, S, lru_w), BF16),
        scratch_shapes=[pltpu.VMEM((SUBLANES, cw), F32), pltpu.VMEM((1, cw), F32)],
        compiler_params=_params("parallel", "parallel", "arbitrary"),
        name="rg_lru",
    )(z, z, conv_w, row(conv_b), w_a.astype(BF16), row(b_a), w_x.astype(BF16), row(b_x),
      row(lru_lambda))


def _rope_kernel(pos_ref, f_ref, sg_ref, cos_ref, sin_ref):
    ang = pos_ref[0].astype(F32) * f_ref[...]
    cos_ref[0] = jnp.cos(ang)
    sin_ref[0] = jnp.sin(ang) * sg_ref[...]


def _rope_tables(positions, head_dim):
    B, S = positions.shape
    half = head_dim // 2
    inv_freq = ROPE_THETA ** (-jnp.arange(0, head_dim, 2, dtype=F32) / head_dim)
    freq_row = jnp.tile(inv_freq, LANES // half).reshape(1, LANES)
    sign_row = jnp.tile(jnp.concatenate([-jnp.ones(half, F32), jnp.ones(half, F32)]),
                        LANES // head_dim).reshape(1, LANES)
    ts = _pick(S, 1024, SUBLANES)
    tab = jax.ShapeDtypeStruct((B, S, LANES), F32)
    return pl.pallas_call(
        _rope_kernel,
        grid=(B, S // ts),
        in_specs=[pl.BlockSpec((1, ts, 1), lambda b, s: (b, s, 0)),
                  pl.BlockSpec((1, LANES), lambda b, s: (0, 0)),
                  pl.BlockSpec((1, LANES), lambda b, s: (0, 0))],
        out_specs=[pl.BlockSpec((1, ts, LANES), lambda b, s: (b, s, 0))] * 2,
        out_shape=[tab, tab],
        compiler_params=_params("parallel", "parallel"),
        name="rope_tables",
    )(positions.reshape(B, S, 1), freq_row, sign_row)


def _qk_prep_kernel(q_ref, k_ref, cos_ref, sin_ref, gq_ref, gk_ref, p_ref, qo_ref, ko_ref,
                    *, nheads, head_dim, q_scale):
    cos = cos_ref[0]
    sin = sin_ref[0]
    pmat = p_ref[...]
    lane = lax.broadcasted_iota(jnp.int32, cos.shape, 1)
    first_half = (lane % head_dim) < (head_dim // 2)

    def prep(t, g):
        ms = jnp.dot(t * t, pmat, preferred_element_type=F32, precision=HIGHEST)
        tn = t * lax.rsqrt(ms + EPS) * g
        swapped = jnp.where(first_half, pltpu.roll(tn, LANES - head_dim // 2, 1),
                            pltpu.roll(tn, head_dim // 2, 1))
        return tn * cos + swapped * sin

    for h in range(nheads):
        sl = slice(h * LANES, (h + 1) * LANES)
        qo_ref[0, :, sl] = (prep(q_ref[0, :, sl].astype(F32), gq_ref[...]) * q_scale).astype(BF16)
        ko_ref[0, :, sl] = prep(k_ref[0, :, sl].astype(F32), gk_ref[...]).astype(BF16)


def _qk_prep(z, cos_t, sin_t, q_norm_g, k_norm_g, q_col, k_col, attn_w, head_dim):
    B, S, _ = z.shape
    assert 2 * head_dim == LANES
    nheads = attn_w // LANES
    ts = _pick(S, 256, SUBLANES)
    qb, kb = q_col // attn_w, k_col // attn_w
    lane = jnp.arange(LANES)
    pmat = (lane[:, None] // head_dim == lane[None, :] // head_dim).astype(F32) / head_dim
    out = jax.ShapeDtypeStruct((B, S, attn_w), BF16)
    tab_spec = pl.BlockSpec((1, ts, LANES), lambda b, s: (b, s, 0))
    row_spec = pl.BlockSpec((1, LANES), lambda b, s: (0, 0))
    return pl.pallas_call(
        functools.partial(_qk_prep_kernel, nheads=nheads, head_dim=head_dim,
                          q_scale=head_dim ** -0.5 * LOG2_E),
        grid=(B, S // ts),
        in_specs=[pl.BlockSpec((1, ts, attn_w), lambda b, s: (b, s, qb)),
                  pl.BlockSpec((1, ts, attn_w), lambda b, s: (b, s, kb)),
                  tab_spec, tab_spec, row_spec, row_spec,
                  pl.BlockSpec((LANES, LANES), lambda b, s: (0, 0))],
        out_specs=[pl.BlockSpec((1, ts, attn_w), lambda b, s: (b, s, 0))] * 2,
        out_shape=[out, out],
        compiler_params=_params("parallel", "parallel"),
        name="qk_prep",
    )(z, z, cos_t, sin_t, jnp.tile(q_norm_g, 2).reshape(1, LANES),
      jnp.tile(k_norm_g, 2).reshape(1, LANES), pmat)


def _attn_kernel(q_ref, k_ref, v_ref, lams_ref, g_ref, o_ref, qq_sc, m_sc, acc_sc,
                 *, tq, tk, rc, head_dim, lam_init):
    qi = pl.program_id(2)
    q = q_ref[0]
    lane = lax.broadcasted_iota(jnp.int32, q.shape, 1)
    zero = jnp.zeros_like(q)
    qq_sc[0:tq] = jnp.where(lane < head_dim, q, zero)
    qq_sc[tq:2 * tq] = jnp.where(lane >= head_dim, q, zero)
    m_sc[...] = jnp.full_like(m_sc, NEG_INF)
    acc_sc[...] = jnp.zeros_like(acc_sc)
    ones = jnp.ones((tk, LANES), BF16)

    def block(ki, diagonal):
        start = pl.multiple_of(ki * tk, tk)
        k = k_ref[0, pl.ds(start, tk), :]
        v1 = jnp.concatenate([v_ref[0, pl.ds(start, tk), :], ones], axis=1)
        for c in range(2 * tq // rc):
            rows = slice(c * rc, (c + 1) * rc)
            ncol = (c * rc) % tq + rc if diagonal else tk
            s = _dot_nt(qq_sc[rows], k[:ncol])
            if diagonal:
                row = lax.broadcasted_iota(jnp.int32, s.shape, 0) + (c * rc) % tq
                col = lax.broadcasted_iota(jnp.int32, s.shape, 1)
                s = jnp.where(col <= row, s, NEG_INF)
            m_prev = m_sc[rows]
            m_new = jnp.maximum(m_prev, jnp.max(s, axis=1, keepdims=True))
            alpha = jnp.exp2(m_prev - m_new)
            p = jnp.exp2(s - jnp.concatenate([m_new] * (ncol // LANES), axis=1)).astype(BF16)
            pv = jnp.dot(p, v1[:ncol], preferred_element_type=F32)
            acc_sc[rows] = jnp.concatenate([alpha, alpha], axis=1) * acc_sc[rows] + pv
            m_sc[rows] = m_new

    def off_diagonal_pair(i, carry):
        block(2 * i, False)
        block(2 * i + 1, False)
        return carry

    lax.fori_loop(0, qi // 2, off_diagonal_pair, 0)

    @pl.when(qi % 2 == 1)
    def _():
        block(qi - 1, False)

    block(qi, True)

    acc = acc_sc[...]
    o = acc[:, :LANES] / acc[:, LANES:]
    lv = lams_ref[...]
    lam = (jnp.exp(jnp.sum(lv[0:1] * lv[1:2], axis=1, keepdims=True))
           - jnp.exp(jnp.sum(lv[2:3] * lv[3:4], axis=1, keepdims=True)) + lam_init)
    d = o[0:tq] - lam * o[tq:2 * tq]
    ms = jnp.mean(d * d, axis=-1, keepdims=True)
    y = d * lax.rsqrt(ms + EPS) * g_ref[...] * (1.0 - lam_init)
    o_ref[0] = y.astype(o_ref.dtype)


def _attention(qn, kn, z, lams, subln_g, v_col, head_dim, lam_init):
    B, S, attn_w = qn.shape
    nheads = attn_w // LANES
    tq = tk = _pick(S, ATTN_BLOCK)
    vb = v_col // LANES
    return pl.pallas_call(
        functools.partial(_attn_kernel, tq=tq, tk=tk, rc=_pick(tq, ATTN_ROW_CHUNK),
                          head_dim=head_dim, lam_init=lam_init),
        grid=(B, nheads, S // tq),
        in_specs=[pl.BlockSpec((1, tq, LANES), lambda b, h, i: (b, i, h)),
                  pl.BlockSpec((1, S, LANES), lambda b, h, i: (b, 0, h)),
                  pl.BlockSpec((1, S, LANES), lambda b, h, i: (b, 0, vb + h)),
                  pl.BlockSpec((4, head_dim), lambda b, h, i: (0, 0)),
                  pl.BlockSpec((1, LANES), lambda b, h, i: (0, 0))],
        out_specs=pl.BlockSpec((1, tq, LANES), lambda b, h, i: (b, i, h)),
        out_shape=jax.ShapeDtypeStruct((B, S, attn_w), BF16),
        scratch_shapes=[pltpu.VMEM((2 * tq, LANES), BF16), pltpu.VMEM((2 * tq, LANES), F32),
                        pltpu.VMEM((2 * tq, 2 * LANES), F32)],
        compiler_params=_params("parallel", "parallel", "parallel"),
        name="diff_attention",
    )(qn, kn, z, lams, subln_g.reshape(1, LANES))


def _merge_kernel(rec_ref, att_ref, wl_ref, wa_ref, gl_ref, ga_ref, o_ref):
    yl = jnp.dot(rec_ref[...], wl_ref[...], preferred_element_type=F32)
    ya = jnp.dot(att_ref[...], wa_ref[...], preferred_element_type=F32)
    gl = jax.nn.sigmoid(gl_ref[...].astype(F32))
    ga = jax.nn.sigmoid(ga_ref[...].astype(F32))
    o_ref[...] = (gl * yl + ga * ya).astype(o_ref.dtype)


def _merge(rec, att, w_lru_out, w_attn_out, z2, gl_col, ga_col):
    T, lru_w = rec.shape
    attn_w = att.shape[1]
    D = w_lru_out.shape[1]
    tm, tn = _pick(T, 1024), _pick(math.gcd(D, gl_col, ga_col), 512)
    glb, gab = gl_col // tn, ga_col // tn
    return pl.pallas_call(
        _merge_kernel,
        grid=(T // tm, D // tn),
        in_specs=[pl.BlockSpec((tm, lru_w), lambda i, j: (i, 0)),
                  pl.BlockSpec((tm, attn_w), lambda i, j: (i, 0)),
                  pl.BlockSpec((lru_w, tn), lambda i, j: (0, j)),
                  pl.BlockSpec((attn_w, tn), lambda i, j: (0, j)),
                  pl.BlockSpec((tm, tn), lambda i, j: (i, glb + j)),
                  pl.BlockSpec((tm, tn), lambda i, j: (i, gab + j))],
        out_specs=pl.BlockSpec((tm, tn), lambda i, j: (i, j)),
        out_shape=jax.ShapeDtypeStruct((T, D), BF16),
        compiler_params=_params("parallel", "parallel"),
        name="merge",
    )(rec, att, w_lru_out, w_attn_out, z2, z2)


def _proj_res_kernel(a_ref, w_ref, x_ref, g_ref, o_ref):
    y = jnp.dot(a_ref[...], w_ref[...], preferred_element_type=F32)
    o_ref[...] = x_ref[...] + g_ref[0] * y


def _proj_residual(a, w, x2, gate, seq):
    T, K = a.shape
    D = w.shape[1]
    tm, tn = _pick(math.gcd(T, seq), 1024), _pick(D, 512)
    per_b = seq // tm
    return pl.pallas_call(
        _proj_res_kernel,
        grid=(T // tm, D // tn),
        in_specs=[pl.BlockSpec((tm, K), lambda i, j: (i, 0)),
                  pl.BlockSpec((K, tn), lambda i, j: (0, j)),
                  pl.BlockSpec((tm, tn), lambda i, j: (i, j)),
                  pl.BlockSpec((1, 1, tn), lambda i, j: (i // per_b, 0, j))],
        out_specs=pl.BlockSpec((tm, tn), lambda i, j: (i, j)),
        out_shape=jax.ShapeDtypeStruct((T, D), F32),
        compiler_params=_params("parallel", "parallel"),
        name="proj_residual",
    )(a, w, x2, gate)


def _topk_rows(s, extras, k):
    nrows = s.shape[0]
    rowf = lax.broadcasted_iota(jnp.int32, s.shape, 0).astype(F32)
    vals, picked = [], [[] for _ in extras]
    idxs = []
    rank = jnp.full(s.shape, float(k), F32)
    for r in range(k):
        m = jnp.max(s, axis=0, keepdims=True)
        idx = jnp.min(jnp.where(s == m, rowf, float(nrows)), axis=0, keepdims=True)
        sel = rowf == idx
        vals.append(m)
        idxs.append(idx)
        for lst, arr in zip(picked, extras):
            lst.append(jnp.max(jnp.where(sel, arr, -1.0), axis=0, keepdims=True))
        s = jnp.where(sel, -jnp.inf, s)
        rank = jnp.where(sel, float(r), rank)
    return vals, idxs, picked, rank


def _stack_rows(rows):
    k, n = len(rows), rows[0].shape[1]
    row = lax.broadcasted_iota(jnp.int32, (k, n), 0)
    out = jnp.zeros((k, n), F32)
    for r, v in enumerate(rows):
        out = jnp.where(row == r, v, out)
    return out


def _topk_kernel(q_ref, keys_ref, kap_ref, cnt_ref, e2_ref, rk2_ref, *, k):
    q = q_ref[...]
    half = q.shape[1] // 2
    s1 = _dot_nt(keys_ref[0], q[:, :half], HIGHEST)
    s2 = _dot_nt(keys_ref[1], q[:, half:], HIGHEST)
    v1, i1, _, _ = _topk_rows(s1, (), k)
    v2, _, _, rk2 = _topk_rows(s2, (), k)
    v1s, i1s = _stack_rows(v1), _stack_rows(i1)
    v2s = _stack_rows(v2)
    n = q.shape[0]
    vals, cas = [], []
    for i in range(k // 2):
        cnt = k // (i + 1)
        rows = -(-cnt // SUBLANES) * SUBLANES
        grp = v1[i] + v2s[0:rows]
        if cnt < rows:
            jrow = lax.broadcasted_iota(jnp.int32, grp.shape, 0)
            grp = jnp.where(jrow < cnt, grp, -jnp.inf)
        vals.append(grp)
        cas.append(jnp.broadcast_to(i1[i], (rows, n)))
    vals.append(v1s[k // 2:k] + v2[0])
    cas.append(i1s[k // 2:k])
    best, _, (a_sel,), _ = _topk_rows(jnp.concatenate(vals, axis=0),
                                      (jnp.concatenate(cas, axis=0),), k)
    z = jnp.sum(jnp.exp(_stack_rows(best) - best[0]), axis=0, keepdims=True)
    keyf = lax.broadcasted_iota(jnp.int32, s1.shape, 0).astype(F32)
    count = jnp.zeros(s1.shape, F32)
    for a in a_sel:
        count = count + jnp.where(keyf == a, 1.0, 0.0)
    kap_ref[...] = jnp.exp(s1 - v1[0]) / z
    cnt_ref[...] = count
    e2_ref[...] = jnp.exp(s2 - v2[0])
    rk2_ref[...] = rk2


def _peer_topk(q, sub_keys, nheads):
    T = q.shape[0]
    nkeys, half = sub_keys.shape[1], sub_keys.shape[2]
    assert half == LANES and nkeys == LANES
    k = PEER_TOPK
    assert k % (2 * SUBLANES) == 0
    tt = _pick(T, 256)
    out = jax.ShapeDtypeStruct((nheads * nkeys, T), F32)
    out_spec = pl.BlockSpec((nkeys, tt), lambda i, h: (h, i))
    return pl.pallas_call(
        functools.partial(_topk_kernel, k=k),
        grid=(T // tt, nheads),
        in_specs=[pl.BlockSpec((tt, 2 * half), lambda i, h: (i, h)),
                  pl.BlockSpec((2, nkeys, half), lambda i, h: (0, 0, 0))],
        out_specs=[out_spec] * 4,
        out_shape=[out] * 4,
        compiler_params=_params("parallel", "parallel"),
        name="peer_topk",
    )(q, sub_keys)


def _peer_kernel(h_ref, u_ref, vt_ref, kap_ref, cnt_ref, e2_ref, rk2_ref, o_ref, w_sc,
                 *, nheads, nkeys):
    j = pl.program_id(1)
    nj = pl.num_programs(1) - 1
    ka = u_ref.shape[0] // nkeys

    def weights():
        parts = []
        for aa in range(ka):
            a = j * ka + aa
            g = None
            for h in range(nheads):
                kap = kap_ref[pl.ds(h * nkeys + a, 1), :]
                cnt = cnt_ref[pl.ds(h * nkeys + a, 1), :]
                e2 = e2_ref[h * nkeys:(h + 1) * nkeys, :]
                rk2 = rk2_ref[h * nkeys:(h + 1) * nkeys, :]
                term = jnp.where(rk2 < cnt, e2, 0.0) * kap
                g = term if g is None else g + term
            parts.append(g)
        gates = jnp.concatenate(parts, axis=0)
        s = _dot_nt(u_ref[...], h_ref[...])
        return (_gelu(s) * gates).astype(BF16)

    cur = j % 2

    @pl.when(j == 0)
    def _():
        o_ref[...] = jnp.zeros_like(o_ref)
        w_sc[0] = weights()

    @pl.when(jnp.logical_and(j > 0, j < nj))
    def _():
        w_next = weights()
        o_ref[...] += jnp.dot(vt_ref[...], w_sc[1 - cur], preferred_element_type=F32)
        w_sc[cur] = w_next

    @pl.when(j == nj)
    def _():
        o_ref[...] += jnp.dot(vt_ref[...], w_sc[1 - cur], preferred_element_type=F32)


def _peer_tiles(T, E):
    return _pick(T, 512), _pick(E, 512)


def _peer_dense(h2, u_tab, vt_tiles, kap, cnt, e2, rk2, nheads, nkeys):
    T, D = h2.shape
    E = u_tab.shape[0]
    tm, te = _peer_tiles(T, E)
    assert te % nkeys == 0 and vt_tiles.shape == (E // te, D, te)
    once = pl.Buffered(1)
    tab_spec = pl.BlockSpec((nheads * nkeys, tm), lambda i, j: (0, i), pipeline_mode=once)
    nj = E // te
    return pl.pallas_call(
        functools.partial(_peer_kernel, nheads=nheads, nkeys=nkeys),
        grid=(T // tm, nj + 1),
        in_specs=[pl.BlockSpec((tm, D), lambda i, j: (i, 0), pipeline_mode=once),
                  pl.BlockSpec((te, D), lambda i, j: (jnp.minimum(j, nj - 1), 0)),
                  pl.BlockSpec((None, D, te), lambda i, j: (jnp.maximum(j - 1, 0), 0, 0)),
                  tab_spec, tab_spec, tab_spec, tab_spec],
        out_specs=pl.BlockSpec((D, tm), lambda i, j: (0, i)),
        out_shape=jax.ShapeDtypeStruct((D, T), F32),
        scratch_shapes=[pltpu.VMEM((2, te, tm), BF16)],
        compiler_params=_params("parallel", "arbitrary"),
        name="peer_dense",
    )(h2, u_tab, vt_tiles, kap, cnt, e2, rk2)


def _residual_kernel(x_ref, yt_ref, g_ref, o_ref):
    o_ref[0] = x_ref[0] + g_ref[0] * yt_ref[...].T


def _residual(x, y_t, gate):
    B, S, D = x.shape
    ts = _pick(S, 256)
    per_b = S // ts
    blk = pl.BlockSpec((1, ts, D), lambda b, s: (b, s, 0))
    return pl.pallas_call(
        _residual_kernel,
        grid=(B, per_b),
        in_specs=[blk, pl.BlockSpec((D, ts), lambda b, s: (0, b * per_b + s)),
                  pl.BlockSpec((1, 1, D), lambda b, s: (b, 0, 0))],
        out_specs=blk,
        out_shape=jax.ShapeDtypeStruct((B, S, D), F32),
        compiler_params=_params("parallel", "parallel"),
        name="residual",
    )(x, y_t, gate)


def kernel(x, c, positions, norm1_g, norm2_g, w_ada, b_ada, w_in, conv_w, conv_b, w_a, b_a, w_x,
           b_x, lru_lambda, q_norm_g, k_norm_g, lambda_q1, lambda_k1, lambda_q2, lambda_k2,
           subln_g, w_lru_out, w_attn_out, w_o, w_pq, sub_keys, u_tab, v_tab):
    B, S, D = x.shape
    T = B * S
    depth = w_in.shape[0]
    lru_w = conv_w.shape[-1]
    attn_w = w_attn_out.shape[1]
    head_dim = q_norm_g.shape[-1]
    nkeys = sub_keys.shape[2]
    peer_heads = w_pq.shape[2] // (2 * sub_keys.shape[3])
    assert B <= SUBLANES and subln_g.shape[-1] == LANES
    q_col = 2 * lru_w
    k_col = q_col + attn_w
    v_col = k_col + attn_w
    gl_col = v_col + attn_w
    ga_col = gl_col + D

    c_pad = jnp.pad(c, ((0, SUBLANES - B), (0, 0)))
    mod = _ada(c_pad, w_ada, b_ada)[:, :B].reshape(depth, B, N_MOD, 1, D)
    cos_t, sin_t = _rope_tables(positions, head_dim)

    for l in range(depth):
        lam_init = 0.8 - 0.6 * math.exp(-0.3 * l)
        shift1, scale1, gate1 = mod[l, :, 0], mod[l, :, 1], mod[l, :, 2]
        shift2, scale2, gate2 = mod[l, :, 3], mod[l, :, 4], mod[l, :, 5]

        h = _norm_mod(x, norm1_g[l], scale1, shift1)
        z2 = _matmul(h.reshape(T, D), w_in[l].astype(BF16), BF16, "in_proj")
        z = z2.reshape(B, S, -1)
        rec = _lru(z, conv_w[l], conv_b[l], w_a[l], b_a[l], w_x[l], b_x[l], lru_lambda[l], lru_w)
        qn, kn = _qk_prep(z, cos_t, sin_t, q_norm_g[l], k_norm_g[l], q_col, k_col, attn_w, head_dim)
        lams = jnp.stack([lambda_q1[l], lambda_k1[l], lambda_q2[l], lambda_k2[l]])
        att = _attention(qn, kn, z, lams, subln_g[l], v_col, head_dim, lam_init)
        merged = _merge(rec.reshape(T, lru_w), att.reshape(T, attn_w), w_lru_out[l].astype(BF16),
                        w_attn_out[l].astype(BF16), z2, gl_col, ga_col)
        x = _proj_residual(merged, w_o[l].astype(BF16), x.reshape(T, D), gate1, S).reshape(B, S, D)

        h2 = _norm_mod(x, norm2_g[l], scale2, shift2).reshape(T, D)
        pq = _matmul(h2, w_pq[l].astype(BF16), F32, "peer_query")
        kap, cnt, e2, rk2 = _peer_topk(pq, sub_keys[l], peer_heads)
        te = _peer_tiles(T, u_tab.shape[1])[1]
        vt_tiles = v_tab[l].astype(BF16).reshape(-1, te, D).transpose(0, 2, 1)
        y_t = _peer_dense(h2, u_tab[l].astype(BF16), vt_tiles, kap, cnt, e2, rk2,
                          peer_heads, nkeys)
        x = _residual(x, y_t, gate2)
    return x
```

```python
import functools
import math

import jax
import jax.numpy as jnp
from jax import lax
from jax.experimental import pallas as pl
from jax.experimental.pallas import tpu as pltpu

F32 = jnp.float32
BF16 = jnp.bfloat16

LANES = 128
SUBLANES = 8
VMEM_LIMIT = 56 * 1024 * 1024
EPS = 1e-6
NEG_INF = -1e30
LRU_C = 8.0
ROPE_THETA = 10000.0
PEER_TOPK = 16
N_MOD = 6
INV_SQRT2 = 0.7071067811865476
LOG2_E = 1.4426950408889634
ATTN_BLOCK = 1024
ATTN_ROW_CHUNK = 256
HIGHEST = lax.Precision.HIGHEST


def _pick(n, pref, align=LANES):
    if n <= pref:
        return n
    t = (pref // align) * align
    while t > align and n % t:
        t -= align
    assert n % t == 0, (n, pref, align)
    return t


def _params(*sem):
    return pltpu.CompilerParams(dimension_semantics=sem, vmem_limit_bytes=VMEM_LIMIT)


def _dot_nt(a, b, precision=None):
    return lax.dot_general(a, b, (((1,), (1,)), ((), ())), precision=precision,
                           preferred_element_type=F32)


def _gelu(x):
    return 0.5 * x * (1.0 + lax.erf(x * INV_SQRT2))


def _ada_kernel(c_ref, w_ref, b_ref, o_ref):
    c = c_ref[...]
    ca = c * jax.nn.sigmoid(c)
    o_ref[0] = jnp.dot(ca, w_ref[0], preferred_element_type=F32, precision=HIGHEST) + b_ref[0]


def _ada(c_pad, w_ada, b_ada):
    L, D, N = w_ada.shape
    tn = _pick(N, 512)
    return pl.pallas_call(
        _ada_kernel,
        grid=(L, N // tn),
        in_specs=[pl.BlockSpec((SUBLANES, D), lambda l, j: (0, 0)),
                  pl.BlockSpec((1, D, tn), lambda l, j: (l, 0, j)),
                  pl.BlockSpec((1, 1, tn), lambda l, j: (l, 0, j))],
        out_specs=pl.BlockSpec((1, SUBLANES, tn), lambda l, j: (l, 0, j)),
        out_shape=jax.ShapeDtypeStruct((L, SUBLANES, N), F32),
        compiler_params=_params("parallel", "parallel"),
        name="adaln",
    )(c_pad, w_ada, b_ada.reshape(L, 1, N))


def _norm_mod_kernel(x_ref, g_ref, sc_ref, sh_ref, o_ref):
    x = x_ref[0]
    ms = jnp.mean(x * x, axis=-1, keepdims=True)
    y = x * lax.rsqrt(ms + EPS) * g_ref[...]
    o_ref[0] = (y * (1.0 + sc_ref[0]) + sh_ref[0]).astype(o_ref.dtype)


def _norm_mod(x, g, scale, shift):
    B, S, D = x.shape
    ts = _pick(S, 256)
    return pl.pallas_call(
        _norm_mod_kernel,
        grid=(B, S // ts),
        in_specs=[pl.BlockSpec((1, ts, D), lambda b, s: (b, s, 0)),
                  pl.BlockSpec((1, D), lambda b, s: (0, 0)),
                  pl.BlockSpec((1, 1, D), lambda b, s: (b, 0, 0)),
                  pl.BlockSpec((1, 1, D), lambda b, s: (b, 0, 0))],
        out_specs=pl.BlockSpec((1, ts, D), lambda b, s: (b, s, 0)),
        out_shape=jax.ShapeDtypeStruct((B, S, D), BF16),
        compiler_params=_params("parallel", "parallel"),
        name="norm_mod",
    )(x, g.reshape(1, D), scale, shift)


def _mm_kernel(a_ref, b_ref, o_ref):
    o_ref[...] = jnp.dot(a_ref[...], b_ref[...], preferred_element_type=F32).astype(o_ref.dtype)


def _matmul(a, b, out_dtype, name):
    M, K = a.shape
    _, N = b.shape
    tm, tn = _pick(M, 1024), _pick(N, 1024)
    return pl.pallas_call(
        _mm_kernel,
        grid=(M // tm, N // tn),
        in_specs=[pl.BlockSpec((tm, K), lambda i, j: (i, 0)),
                  pl.BlockSpec((K, tn), lambda i, j: (0, j))],
        out_specs=pl.BlockSpec((tm, tn), lambda i, j: (i, j)),
        out_shape=jax.ShapeDtypeStruct((M, N), out_dtype),
        compiler_params=_params("parallel", "parallel"),
        name=name,
    )(a, b)


def _shift_rows(x, s, fill):
    row = lax.broadcasted_iota(jnp.int32, x.shape, 0)
    return jnp.where(row >= s, pltpu.roll(x, s, 0), fill)


def _linear_scan(a, u):
    n = a.shape[0]
    s = 1
    while s < n:
        u = a * _shift_rows(u, s, 0.0) + u
        a = a * _shift_rows(a, s, 1.0)
        s *= 2
    return a, u


def _lru_kernel(x_ref, gate_ref, cw_ref, cb_ref, wa_ref, ba_ref, wx_ref, bx_ref, lam_ref,
                o_ref, tail_sc, h_sc, *, nblk):
    @pl.when(pl.program_id(2) == 0)
    def _():
        tail_sc[...] = jnp.zeros_like(tail_sc)
        h_sc[...] = jnp.zeros_like(h_sc)

    ts = x_ref.shape[1]
    row8 = lax.broadcasted_iota(jnp.int32, (SUBLANES, LANES), 0)
    for j in range(nblk):
        sl = slice(j * LANES, (j + 1) * LANES)
        x = x_ref[0, :, sl].astype(F32)
        tail = tail_sc[:, sl]
        cw = cw_ref[:, sl]
        nk = cw.shape[0]
        xc = cb_ref[:, sl] + cw[nk - 1:nk] * x
        for d in range(1, nk):
            xr = pltpu.roll(x, d, 0)
            head = jnp.where(row8 < d, pltpu.roll(tail, d, 0), xr[:SUBLANES])
            xd = jnp.concatenate([head, xr[SUBLANES:]], axis=0)
            xc = xc + cw[nk - 1 - d:nk - d] * xd
        tail_sc[:, sl] = x[ts - SUBLANES:]

        xcb = xc.astype(BF16)
        r = jax.nn.sigmoid(jnp.dot(xcb, wa_ref[j], preferred_element_type=F32) + ba_ref[:, sl])
        i = jax.nn.sigmoid(jnp.dot(xcb, wx_ref[j], preferred_element_type=F32) + bx_ref[:, sl])
        neg_lam = -lam_ref[:, sl]
        softplus = jnp.maximum(neg_lam, 0.0) + jnp.log1p(jnp.exp(-jnp.abs(neg_lam)))
        log_a = (-LRU_C) * r * softplus
        a = jnp.exp(log_a)
        u = jnp.sqrt(1.0 - a * a) * (i * xc)
        acum, h = _linear_scan(a, u)
        h = h + acum * h_sc[:, sl]
        h_sc[:, sl] = h[ts - 1:ts]
        gate = gate_ref[0, :, sl].astype(F32)
        o_ref[0, :, sl] = (h * _gelu(gate)).astype(o_ref.dtype)


def _lru(z, conv_w, conv_b, w_a, b_a, w_x, b_x, lru_lambda, lru_w):
    B, S, _ = z.shape
    nblocks = w_a.shape[0]
    assert w_a.shape[1] == LANES and lru_w == nblocks * LANES
    cw = _pick(lru_w, 512)
    nblk = cw // LANES
    ts = _pick(S, 512, SUBLANES)
    ncb = lru_w // cw
    row = lambda v: v.reshape(1, lru_w)
    vec_spec = pl.BlockSpec((1, cw), lambda b, c, s: (0, c))
    return pl.pallas_call(
        functools.partial(_lru_kernel, nblk=nblk),
        grid=(B, ncb, S // ts),
        in_specs=[pl.BlockSpec((1, ts, cw), lambda b, c, s: (b, s, c)),
                  pl.BlockSpec((1, ts, cw), lambda b, c, s: (b, s, c + ncb)),
                  pl.BlockSpec((conv_w.shape[0], cw), lambda b, c, s: (0, c)),
                  vec_spec,
                  pl.BlockSpec((nblk, LANES, LANES), lambda b, c, s: (c, 0, 0)),
                  vec_spec,
                  pl.BlockSpec((nblk, LANES, LANES), lambda b, c, s: (c, 0, 0)),
                  vec_spec, vec_spec],
        out_specs=pl.BlockSpec((1, ts, cw), lambda b, c, s: (b, s, c)),
        out_shape=jax.ShapeDtypeStruct((B, S, lru_w), BF16),
        scratch_shapes=[pltpu.VMEM((SUBLANES, cw), F32), pltpu.VMEM((1, cw), F32)],
        compiler_params=_params("parallel", "parallel", "arbitrary"),
        name="rg_lru",
    )(z, z, conv_w, row(conv_b), w_a.astype(BF16), row(b_a), w_x.astype(BF16), row(b_x),
      row(lru_lambda))


def _rope_kernel(pos_ref, f_ref, sg_ref, cos_ref, sin_ref):
    ang = pos_ref[0].astype(F32) * f_ref[...]
    cos_ref[0] = jnp.cos(ang)
    sin_ref[0] = jnp.sin(ang) * sg_ref[...]


def _rope_tables(positions, head_dim):
    B, S = positions.shape
    half = head_dim // 2
    inv_freq = ROPE_THETA ** (-jnp.arange(0, head_dim, 2, dtype=F32) / head_dim)
    freq_row = jnp.tile(inv_freq, LANES // half).reshape(1, LANES)
    sign_row = jnp.tile(jnp.concatenate([-jnp.ones(half, F32), jnp.ones(half, F32)]),
                        LANES // head_dim).reshape(1, LANES)
    ts = _pick(S, 1024, SUBLANES)
    tab = jax.ShapeDtypeStruct((B, S, LANES), F32)
    return pl.pallas_call(
        _rope_kernel,
        grid=(B, S // ts),
        in_specs=[pl.BlockSpec((1, ts, 1), lambda b, s: (b, s, 0)),
                  pl.BlockSpec((1, LANES), lambda b, s: (0, 0)),
                  pl.BlockSpec((1, LANES), lambda b, s: (0, 0))],
        out_specs=[pl.BlockSpec((1, ts, LANES), lambda b, s: (b, s, 0))] * 2,
        out_shape=[tab, tab],
        compiler_params=_params("parallel", "parallel"),
        name="rope_tables",
    )(positions.reshape(B, S, 1), freq_row, sign_row)


def _qk_prep_kernel(q_ref, k_ref, cos_ref, sin_ref, gq_ref, gk_ref, p_ref, qo_ref, ko_ref,
                    *, nheads, head_dim, q_scale):
    cos = cos_ref[0]
    sin = sin_ref[0]
    pmat = p_ref[...]
    lane = lax.broadcasted_iota(jnp.int32, cos.shape, 1)
    first_half = (lane % head_dim) < (head_dim // 2)

    def prep(t, g):
        ms = jnp.dot(t * t, pmat, preferred_element_type=F32, precision=HIGHEST)
        tn = t * lax.rsqrt(ms + EPS) * g
        swapped = jnp.where(first_half, pltpu.roll(tn, LANES - head_dim // 2, 1),
                            pltpu.roll(tn, head_dim // 2, 1))
        return tn * cos + swapped * sin

    for h in range(nheads):
        sl = slice(h * LANES, (h + 1) * LANES)
        qo_ref[0, :, sl] = (prep(q_ref[0, :, sl].astype(F32), gq_ref[...]) * q_scale).astype(BF16)
        ko_ref[0, :, sl] = prep(k_ref[0, :, sl].astype(F32), gk_ref[...]).astype(BF16)


def _qk_prep(z, cos_t, sin_t, q_norm_g, k_norm_g, q_col, k_col, attn_w, head_dim):
    B, S, _ = z.shape
    assert 2 * head_dim == LANES
    nheads = attn_w // LANES
    ts = _pick(S, 256, SUBLANES)
    qb, kb = q_col // attn_w, k_col // attn_w
    lane = jnp.arange(LANES)
    pmat = (lane[:, None] // head_dim == lane[None, :] // head_dim).astype(F32) / head_dim
    out = jax.ShapeDtypeStruct((B, S, attn_w), BF16)
    tab_spec = pl.BlockSpec((1, ts, LANES), lambda b, s: (b, s, 0))
    row_spec = pl.BlockSpec((1, LANES), lambda b, s: (0, 0))
    return pl.pallas_call(
        functools.partial(_qk_prep_kernel, nheads=nheads, head_dim=head_dim,
                          q_scale=head_dim ** -0.5 * LOG2_E),
        grid=(B, S // ts),
        in_specs=[pl.BlockSpec((1, ts, attn_w), lambda b, s: (b, s, qb)),
                  pl.BlockSpec((1, ts, attn_w), lambda b, s: (b, s, kb)),
                  tab_spec, tab_spec, row_spec, row_spec,
                  pl.BlockSpec((LANES, LANES), lambda b, s: (0, 0))],
        out_specs=[pl.BlockSpec((1, ts, attn_w), lambda b, s: (b, s, 0))] * 2,
        out_shape=[out, out],
        compiler_params=_params("parallel", "parallel"),
        name="qk_prep",
    )(z, z, cos_t, sin_t, jnp.tile(q_norm_g, 2).reshape(1, LANES),
      jnp.tile(k_norm_g, 2).reshape(1, LANES), pmat)


def _attn_kernel(q_ref, k_ref, v_ref, lams_ref, g_ref, o_ref, qq_sc, m_sc, acc_sc,
                 *, tq, tk, rc, head_dim, lam_init):
    qi = pl.program_id(2)
    q = q_ref[0]
    lane = lax.broadcasted_iota(jnp.int32, q.shape, 1)
    zero = jnp.zeros_like(q)
    qq_sc[0:tq] = jnp.where(lane < head_dim, q, zero)
    qq_sc[tq:2 * tq] = jnp.where(lane >= head_dim, q, zero)
    m_sc[...] = jnp.full_like(m_sc, NEG_INF)
    acc_sc[...] = jnp.zeros_like(acc_sc)
    ones = jnp.ones((tk, LANES), BF16)

    def block(ki, diagonal):
        start = pl.multiple_of(ki * tk, tk)
        k = k_ref[0, pl.ds(start, tk), :]
        v1 = jnp.concatenate([v_ref[0, pl.ds(start, tk), :], ones], axis=1)
        for c in range(2 * tq // rc):
            rows = slice(c * rc, (c + 1) * rc)
            ncol = (c * rc) % tq + rc if diagonal else tk
            s = _dot_nt(qq_sc[rows], k[:ncol])
            if diagonal:
                row = lax.broadcasted_iota(jnp.int32, s.shape, 0) + (c * rc) % tq
                col = lax.broadcasted_iota(jnp.int32, s.shape, 1)
                s = jnp.where(col <= row, s, NEG_INF)
            m_prev = m_sc[rows]
            m_new = jnp.maximum(m_prev, jnp.max(s, axis=1, keepdims=True))
            alpha = jnp.exp2(m_prev - m_new)
            p = jnp.exp2(s - jnp.concatenate([m_new] * (ncol // LANES), axis=1)).astype(BF16)
            pv = jnp.dot(p, v1[:ncol], preferred_element_type=F32)
            acc_sc[rows] = jnp.concatenate([alpha, alpha], axis=1) * acc_sc[rows] + pv
            m_sc[rows] = m_new

    def off_diagonal_pair(i, carry):
        block(2 * i, False)
        block(2 * i + 1, False)
        return carry

    lax.fori_loop(0, qi // 2, off_diagonal_pair, 0)

    @pl.when(qi % 2 == 1)
    def _():
        block(qi - 1, False)

    block(qi, True)

    acc = acc_sc[...]
    o = acc[:, :LANES] / acc[:, LANES:]
    lv = lams_ref[...]
    lam = (jnp.exp(jnp.sum(lv[0:1] * lv[1:2], axis=1, keepdims=True))
           - jnp.exp(jnp.sum(lv[2:3] * lv[3:4], axis=1, keepdims=True)) + lam_init)
    d = o[0:tq] - lam * o[tq:2 * tq]
    ms = jnp.mean(d * d, axis=-1, keepdims=True)
    y = d * lax.rsqrt(ms + EPS) * g_ref[...] * (1.0 - lam_init)
    o_ref[0] = y.astype(o_ref.dtype)


def _attention(qn, kn, z, lams, subln_g, v_col, head_dim, lam_init):
    B, S, attn_w = qn.shape
    nheads = attn_w // LANES
    tq = tk = _pick(S, ATTN_BLOCK)
    vb = v_col // LANES
    return pl.pallas_call(
        functools.partial(_attn_kernel, tq=tq, tk=tk, rc=_pick(tq, ATTN_ROW_CHUNK),
                          head_dim=head_dim, lam_init=lam_init),
        grid=(B, nheads, S // tq),
        in_specs=[pl.BlockSpec((1, tq, LANES), lambda b, h, i: (b, i, h)),
                  pl.BlockSpec((1, S, LANES), lambda b, h, i: (b, 0, h)),
                  pl.BlockSpec((1, S, LANES), lambda b, h, i: (b, 0, vb + h)),
                  pl.BlockSpec((4, head_dim), lambda b, h, i: (0, 0)),
                  pl.BlockSpec((1, LANES), lambda b, h, i: (0, 0))],
        out_specs=pl.BlockSpec((1, tq, LANES), lambda b, h, i: (b, i, h)),
        out_shape=jax.ShapeDtypeStruct((B, S, attn_w), BF16),
        scratch_shapes=[pltpu.VMEM((2 * tq, LANES), BF16), pltpu.VMEM((2 * tq, LANES), F32),
                        pltpu.VMEM((2 * tq, 2 * LANES), F32)],
        compiler_params=_params("parallel", "parallel", "parallel"),
        name="diff_attention",
    )(qn, kn, z, lams, subln_g.reshape(1, LANES))


def _merge_kernel(rec_ref, att_ref, wl_ref, wa_ref, gl_ref, ga_ref, o_ref):
    yl = jnp.dot(rec_ref[...], wl_ref[...], preferred_element_type=F32)
    ya = jnp.dot(att_ref[...], wa_ref[...], preferred_element_type=F32)
    gl = jax.nn.sigmoid(gl_ref[...].astype(F32))
    ga = jax.nn.sigmoid(ga_ref[...].astype(F32))
    o_ref[...] = (gl * yl + ga * ya).astype(o_ref.dtype)


def _merge(rec, att, w_lru_out, w_attn_out, z2, gl_col, ga_col):
    T, lru_w = rec.shape
    attn_w = att.shape[1]
    D = w_lru_out.shape[1]
    tm, tn = _pick(T, 1024), _pick(math.gcd(D, gl_col, ga_col), 512)
    glb, gab = gl_col // tn, ga_col // tn
    return pl.pallas_call(
        _merge_kernel,
        grid=(T // tm, D // tn),
        in_specs=[pl.BlockSpec((tm, lru_w), lambda i, j: (i, 0)),
                  pl.BlockSpec((tm, attn_w), lambda i, j: (i, 0)),
                  pl.BlockSpec((lru_w, tn), lambda i, j: (0, j)),
                  pl.BlockSpec((attn_w, tn), lambda i, j: (0, j)),
                  pl.BlockSpec((tm, tn), lambda i, j: (i, glb + j)),
                  pl.BlockSpec((tm, tn), lambda i, j: (i, gab + j))],
        out_specs=pl.BlockSpec((tm, tn), lambda i, j: (i, j)),
        out_shape=jax.ShapeDtypeStruct((T, D), BF16),
        compiler_params=_params("parallel", "parallel"),
        name="merge",
    )(rec, att, w_lru_out, w_attn_out, z2, z2)


def _proj_res_kernel(a_ref, w_ref, x_ref, g_ref, o_ref):
    y = jnp.dot(a_ref[...], w_ref[...], preferred_element_type=F32)
    o_ref[...] = x_ref[...] + g_ref[0] * y


def _proj_residual(a, w, x2, gate, seq):
    T, K = a.shape
    D = w.shape[1]
    tm, tn = _pick(math.gcd(T, seq), 1024), _pick(D, 512)
    per_b = seq // tm
    return pl.pallas_call(
        _proj_res_kernel,
        grid=(T // tm, D // tn),
        in_specs=[pl.BlockSpec((tm, K), lambda i, j: (i, 0)),
                  pl.BlockSpec((K, tn), lambda i, j: (0, j)),
                  pl.BlockSpec((tm, tn), lambda i, j: (i, j)),
                  pl.BlockSpec((1, 1, tn), lambda i, j: (i // per_b, 0, j))],
        out_specs=pl.BlockSpec((tm, tn), lambda i, j: (i, j)),
        out_shape=jax.ShapeDtypeStruct((T, D), F32),
        compiler_params=_params("parallel", "parallel"),
        name="proj_residual",
    )(a, w, x2, gate)


def _topk_rows(s, extras, k):
    nrows = s.shape[0]
    rowf = lax.broadcasted_iota(jnp.int32, s.shape, 0).astype(F32)
    vals, picked = [], [[] for _ in extras]
    idxs = []
    rank = jnp.full(s.shape, float(k), F32)
    for r in range(k):
        m = jnp.max(s, axis=0, keepdims=True)
        idx = jnp.min(jnp.where(s == m, rowf, float(nrows)), axis=0, keepdims=True)
        sel = rowf == idx
        vals.append(m)
        idxs.append(idx)
        for lst, arr in zip(picked, extras):
            lst.append(jnp.max(jnp.where(sel, arr, -1.0), axis=0, keepdims=True))
        s = jnp.where(sel, -jnp.inf, s)
        rank = jnp.where(sel, float(r), rank)
    return vals, idxs, picked, rank


def _stack_rows(rows):
    k, n = len(rows), rows[0].shape[1]
    row = lax.broadcasted_iota(jnp.int32, (k, n), 0)
    out = jnp.zeros((k, n), F32)
    for r, v in enumerate(rows):
        out = jnp.where(row == r, v, out)
    return out


def _topk_kernel(q_ref, keys_ref, kap_ref, cnt_ref, e2_ref, rk2_ref, *, k):
    q = q_ref[...]
    half = q.shape[1] // 2
    s1 = _dot_nt(keys_ref[0], q[:, :half], HIGHEST)
    s2 = _dot_nt(keys_ref[1], q[:, half:], HIGHEST)
    v1, i1, _, _ = _topk_rows(s1, (), k)
    v2, _, _, rk2 = _topk_rows(s2, (), k)
    v1s, i1s = _stack_rows(v1), _stack_rows(i1)
    v2s = _stack_rows(v2)
    n = q.shape[0]
    vals, cas = [], []
    for i in range(k // 2):
        cnt = k // (i + 1)
        rows = -(-cnt // SUBLANES) * SUBLANES
        grp = v1[i] + v2s[0:rows]
        if cnt < rows:
            jrow = lax.broadcasted_iota(jnp.int32, grp.shape, 0)
            grp = jnp.where(jrow < cnt, grp, -jnp.inf)
        vals.append(grp)
        cas.append(jnp.broadcast_to(i1[i], (rows, n)))
    vals.append(v1s[k // 2:k] + v2[0])
    cas.append(i1s[k // 2:k])
    best, _, (a_sel,), _ = _topk_rows(jnp.concatenate(vals, axis=0),
                                      (jnp.concatenate(cas, axis=0),), k)
    z = jnp.sum(jnp.exp(_stack_rows(best) - best[0]), axis=0, keepdims=True)
    keyf = lax.broadcasted_iota(jnp.int32, s1.shape, 0).astype(F32)
    count = jnp.zeros(s1.shape, F32)
    for a in a_sel:
        count = count + jnp.where(keyf == a, 1.0, 0.0)
    kap_ref[...] = jnp.exp(s1 - v1[0]) / z
    cnt_ref[...] = count
    e2_ref[...] = jnp.exp(s2 - v2[0]).astype(e2_ref.dtype)
    rk2_ref[...] = rk2.astype(rk2_ref.dtype)


def _peer_topk(q, sub_keys, nheads):
    T = q.shape[0]
    nkeys, half = sub_keys.shape[1], sub_keys.shape[2]
    assert half == LANES and nkeys == LANES
    k = PEER_TOPK
    assert k % (2 * SUBLANES) == 0
    tt = _pick(T, 256)
    out = jax.ShapeDtypeStruct((nheads * nkeys, T), F32)
    out_b = jax.ShapeDtypeStruct((nheads * nkeys, T), BF16)
    out_spec = pl.BlockSpec((nkeys, tt), lambda i, h: (h, i))
    return pl.pallas_call(
        functools.partial(_topk_kernel, k=k),
        grid=(T // tt, nheads),
        in_specs=[pl.BlockSpec((tt, 2 * half), lambda i, h: (i, h)),
                  pl.BlockSpec((2, nkeys, half), lambda i, h: (0, 0, 0))],
        out_specs=[out_spec] * 4,
        out_shape=[out, out, out_b, out_b],
        compiler_params=_params("parallel", "parallel"),
        name="peer_topk",
    )(q, sub_keys)


def _peer_kernel(h_ref, u_ref, vt_ref, kap_ref, cnt_ref, e2_ref, rk2_ref, o_ref, w_sc,
                 *, nheads, nkeys):
    j = pl.program_id(1)
    nj = pl.num_programs(1) - 1
    ka = u_ref.shape[0] // nkeys

    def weights():
        parts = []
        for aa in range(ka):
            a = j * ka + aa
            g = None
            for h in range(nheads):
                kap = kap_ref[pl.ds(h * nkeys + a, 1), :].astype(BF16)
                cnt = cnt_ref[pl.ds(h * nkeys + a, 1), :].astype(BF16)
                e2 = e2_ref[h * nkeys:(h + 1) * nkeys, :]
                rk2 = rk2_ref[h * nkeys:(h + 1) * nkeys, :]
                term = jnp.where(rk2 < cnt, e2, jnp.zeros_like(e2)) * kap
                g = term if g is None else g + term
            parts.append(g)
        gates = jnp.concatenate(parts, axis=0)
        s = _dot_nt(u_ref[...], h_ref[...])
        return _gelu(s).astype(BF16) * gates

    cur = j % 2

    @pl.when(j == 0)
    def _():
        o_ref[...] = jnp.zeros_like(o_ref)
        w_sc[0] = weights()

    @pl.when(jnp.logical_and(j > 0, j < nj))
    def _():
        w_next = weights()
        o_ref[...] += jnp.dot(vt_ref[...], w_sc[1 - cur], preferred_element_type=F32)
        w_sc[cur] = w_next

    @pl.when(j == nj)
    def _():
        o_ref[...] += jnp.dot(vt_ref[...], w_sc[1 - cur], preferred_element_type=F32)


def _peer_tiles(T, E):
    return _pick(T, 512), _pick(E, 512)


def _peer_dense(h2, u_tab, vt_tiles, kap, cnt, e2, rk2, nheads, nkeys):
    T, D = h2.shape
    E = u_tab.shape[0]
    tm, te = _peer_tiles(T, E)
    assert te % nkeys == 0 and vt_tiles.shape == (E // te, D, te)
    once = pl.Buffered(1)
    tab_spec = pl.BlockSpec((nheads * nkeys, tm), lambda i, j: (0, i), pipeline_mode=once)
    nj = E // te
    return pl.pallas_call(
        functools.partial(_peer_kernel, nheads=nheads, nkeys=nkeys),
        grid=(T // tm, nj + 1),
        in_specs=[pl.BlockSpec((tm, D), lambda i, j: (i, 0), pipeline_mode=once),
                  pl.BlockSpec((te, D), lambda i, j: (jnp.minimum(j, nj - 1), 0)),
                  pl.BlockSpec((None, D, te), lambda i, j: (jnp.maximum(j - 1, 0), 0, 0)),
                  tab_spec, tab_spec, tab_spec, tab_spec],
        out_specs=pl.BlockSpec((D, tm), lambda i, j: (0, i)),
        out_shape=jax.ShapeDtypeStruct((D, T), F32),
        scratch_shapes=[pltpu.VMEM((2, te, tm), BF16)],
        compiler_params=_params("parallel", "arbitrary"),
        name="peer_dense",
    )(h2, u_tab, vt_tiles, kap, cnt, e2, rk2)


def _residual_kernel(x_ref, yt_ref, g_ref, o_ref):
    o_ref[0] = x_ref[0] + g_ref[0] * yt_ref[...].T


def _residual(x, y_t, gate):
    B, S, D = x.shape
    ts = _pick(S, 256)
    per_b = S // ts
    blk = pl.BlockSpec((1, ts, D), lambda b, s: (b, s, 0))
    return pl.pallas_call(
        _residual_kernel,
        grid=(B, per_b),
        in_specs=[blk, pl.BlockSpec((D, ts), lambda b, s: (0, b * per_b + s)),
                  pl.BlockSpec((1, 1, D), lambda b, s: (b, 0, 0))],
        out_specs=blk,
        out_shape=jax.ShapeDtypeStruct((B, S, D), F32),
        compiler_params=_params("parallel", "parallel"),
        name="residual",
    )(x, y_t, gate)


def kernel(x, c, positions, norm1_g, norm2_g, w_ada, b_ada, w_in, conv_w, conv_b, w_a, b_a, w_x,
           b_x, lru_lambda, q_norm_g, k_norm_g, lambda_q1, lambda_k1, lambda_q2, lambda_k2,
           subln_g, w_lru_out, w_attn_out, w_o, w_pq, sub_keys, u_tab, v_tab):
    B, S, D = x.shape
    T = B * S
    depth = w_in.shape[0]
    lru_w = conv_w.shape[-1]
    attn_w = w_attn_out.shape[1]
    head_dim = q_norm_g.shape[-1]
    nkeys = sub_keys.shape[2]
    peer_heads = w_pq.shape[2] // (2 * sub_keys.shape[3])
    assert B <= SUBLANES and subln_g.shape[-1] == LANES
    q_col = 2 * lru_w
    k_col = q_col + attn_w
    v_col = k_col + attn_w
    gl_col = v_col + attn_w
    ga_col = gl_col + D

    c_pad = jnp.pad(c, ((0, SUBLANES - B), (0, 0)))
    mod = _ada(c_pad, w_ada, b_ada)[:, :B].reshape(depth, B, N_MOD, 1, D)
    cos_t, sin_t = _rope_tables(positions, head_dim)

    for l in range(depth):
        lam_init = 0.8 - 0.6 * math.exp(-0.3 * l)
        shift1, scale1, gate1 = mod[l, :, 0], mod[l, :, 1], mod[l, :, 2]
        shift2, scale2, gate2 = mod[l, :, 3], mod[l, :, 4], mod[l, :, 5]

        h = _norm_mod(x, norm1_g[l], scale1, shift1)
        z2 = _matmul(h.reshape(T, D), w_in[l].astype(BF16), BF16, "in_proj")
        z = z2.reshape(B, S, -1)
        rec = _lru(z, conv_w[l], conv_b[l], w_a[l], b_a[l], w_x[l], b_x[l], lru_lambda[l], lru_w)
        qn, kn = _qk_prep(z, cos_t, sin_t, q_norm_g[l], k_norm_g[l], q_col, k_col, attn_w, head_dim)
        lams = jnp.stack([lambda_q1[l], lambda_k1[l], lambda_q2[l], lambda_k2[l]])
        att = _attention(qn, kn, z, lams, subln_g[l], v_col, head_dim, lam_init)
        merged = _merge(rec.reshape(T, lru_w), att.reshape(T, attn_w), w_lru_out[l].astype(BF16),
                        w_attn_out[l].astype(BF16), z2, gl_col, ga_col)
        x = _proj_residual(merged, w_o[l].astype(BF16), x.reshape(T, D), gate1, S).reshape(B, S, D)

        h2 = _norm_mod(x, norm2_g[l], scale2, shift2).reshape(T, D)
        pq = _matmul(h2, w_pq[l].astype(BF16), F32, "peer_query")
        kap, cnt, e2, rk2 = _peer_topk(pq, sub_keys[l], peer_heads)
        te = _peer_tiles(T, u_tab.shape[1])[1]
        vt_tiles = v_tab[l].astype(BF16).reshape(-1, te, D).transpose(0, 2, 1)
        y_t = _peer_dense(h2, u_tab[l].astype(BF16), vt_tiles, kap, cnt, e2, rk2,
                          peer_heads, nkeys)
        x = _residual(x, y_t, gate2)
    return x
```

```python
import functools
import math

import jax
import jax.numpy as jnp
from jax import lax
from jax.experimental import pallas as pl
from jax.experimental.pallas import tpu as pltpu

F32 = jnp.float32
BF16 = jnp.bfloat16

LANES = 128
SUBLANES = 8
VMEM_LIMIT = 56 * 1024 * 1024
EPS = 1e-6
NEG_INF = -1e30
LRU_C = 8.0
ROPE_THETA = 10000.0
PEER_TOPK = 16
N_MOD = 6
INV_SQRT2 = 0.7071067811865476
LOG2_E = 1.4426950408889634
ATTN_BLOCK = 1024
ATTN_ROW_CHUNK = 512
PEER_TOPK_TOKENS = 512
HIGHEST = lax.Precision.HIGHEST


def _pick(n, pref, align=LANES):
    if n <= pref:
        return n
    t = (pref // align) * align
    while t > align and n % t:
        t -= align
    assert n % t == 0, (n, pref, align)
    return t


def _params(*sem):
    return pltpu.CompilerParams(dimension_semantics=sem, vmem_limit_bytes=VMEM_LIMIT)


def _dot_nt(a, b, precision=None):
    return lax.dot_general(a, b, (((1,), (1,)), ((), ())), precision=precision,
                           preferred_element_type=F32)


def _gelu(x):
    return 0.5 * x * (1.0 + lax.erf(x * INV_SQRT2))


def _ada_kernel(c_ref, w_ref, b_ref, o_ref):
    c = c_ref[...]
    ca = c * jax.nn.sigmoid(c)
    o_ref[0] = jnp.dot(ca, w_ref[0], preferred_element_type=F32, precision=HIGHEST) + b_ref[0]


def _ada(c_pad, w_ada, b_ada):
    L, D, N = w_ada.shape
    tn = _pick(N, 512)
    return pl.pallas_call(
        _ada_kernel,
        grid=(L, N // tn),
        in_specs=[pl.BlockSpec((SUBLANES, D), lambda l, j: (0, 0)),
                  pl.BlockSpec((1, D, tn), lambda l, j: (l, 0, j)),
                  pl.BlockSpec((1, 1, tn), lambda l, j: (l, 0, j))],
        out_specs=pl.BlockSpec((1, SUBLANES, tn), lambda l, j: (l, 0, j)),
        out_shape=jax.ShapeDtypeStruct((L, SUBLANES, N), F32),
        compiler_params=_params("parallel", "parallel"),
        name="adaln",
    )(c_pad, w_ada, b_ada.reshape(L, 1, N))


def _norm_mod_kernel(x_ref, g_ref, sc_ref, sh_ref, o_ref):
    x = x_ref[0]
    ms = jnp.mean(x * x, axis=-1, keepdims=True)
    y = x * lax.rsqrt(ms + EPS) * g_ref[...]
    o_ref[0] = (y * (1.0 + sc_ref[0]) + sh_ref[0]).astype(o_ref.dtype)


def _norm_mod(x, g, scale, shift):
    B, S, D = x.shape
    ts = _pick(S, 256)
    return pl.pallas_call(
        _norm_mod_kernel,
        grid=(B, S // ts),
        in_specs=[pl.BlockSpec((1, ts, D), lambda b, s: (b, s, 0)),
                  pl.BlockSpec((1, D), lambda b, s: (0, 0)),
                  pl.BlockSpec((1, 1, D), lambda b, s: (b, 0, 0)),
                  pl.BlockSpec((1, 1, D), lambda b, s: (b, 0, 0))],
        out_specs=pl.BlockSpec((1, ts, D), lambda b, s: (b, s, 0)),
        out_shape=jax.ShapeDtypeStruct((B, S, D), BF16),
        compiler_params=_params("parallel", "parallel"),
        name="norm_mod",
    )(x, g.reshape(1, D), scale, shift)


def _mm_kernel(a_ref, b_ref, o_ref):
    o_ref[...] = jnp.dot(a_ref[...], b_ref[...], preferred_element_type=F32).astype(o_ref.dtype)


def _matmul(a, b, out_dtype, name):
    M, K = a.shape
    _, N = b.shape
    tm, tn = _pick(M, 1024), _pick(N, 1024)
    return pl.pallas_call(
        _mm_kernel,
        grid=(M // tm, N // tn),
        in_specs=[pl.BlockSpec((tm, K), lambda i, j: (i, 0)),
                  pl.BlockSpec((K, tn), lambda i, j: (0, j))],
        out_specs=pl.BlockSpec((tm, tn), lambda i, j: (i, j)),
        out_shape=jax.ShapeDtypeStruct((M, N), out_dtype),
        compiler_params=_params("parallel", "parallel"),
        name=name,
    )(a, b)


def _shift_rows(x, s, fill):
    row = lax.broadcasted_iota(jnp.int32, x.shape, 0)
    return jnp.where(row >= s, pltpu.roll(x, s, 0), fill)


def _linear_scan(a, u):
    n = a.shape[0]
    s = 1
    while s < n:
        u = a * _shift_rows(u, s, 0.0) + u
        a = a * _shift_rows(a, s, 1.0)
        s *= 2
    return a, u


def _lru_kernel(x_ref, gate_ref, cw_ref, cb_ref, wa_ref, ba_ref, wx_ref, bx_ref, lam_ref,
                o_ref, tail_sc, h_sc, *, nblk):
    @pl.when(pl.program_id(2) == 0)
    def _():
        tail_sc[...] = jnp.zeros_like(tail_sc)
        h_sc[...] = jnp.zeros_like(h_sc)

    ts = x_ref.shape[1]
    row8 = lax.broadcasted_iota(jnp.int32, (SUBLANES, LANES), 0)
    for j in range(nblk):
        sl = slice(j * LANES, (j + 1) * LANES)
        x = x_ref[0, :, sl].astype(F32)
        tail = tail_sc[:, sl]
        cw = cw_ref[:, sl]
        nk = cw.shape[0]
        xc = cb_ref[:, sl] + cw[nk - 1:nk] * x
        for d in range(1, nk):
            xr = pltpu.roll(x, d, 0)
            head = jnp.where(row8 < d, pltpu.roll(tail, d, 0), xr[:SUBLANES])
            xd = jnp.concatenate([head, xr[SUBLANES:]], axis=0)
            xc = xc + cw[nk - 1 - d:nk - d] * xd
        tail_sc[:, sl] = x[ts - SUBLANES:]

        xcb = xc.astype(BF16)
        r = jax.nn.sigmoid(jnp.dot(xcb, wa_ref[j], preferred_element_type=F32) + ba_ref[:, sl])
        i = jax.nn.sigmoid(jnp.dot(xcb, wx_ref[j], preferred_element_type=F32) + bx_ref[:, sl])
        neg_lam = -lam_ref[:, sl]
        softplus = jnp.maximum(neg_lam, 0.0) + jnp.log1p(jnp.exp(-jnp.abs(neg_lam)))
        log_a = (-LRU_C) * r * softplus
        a = jnp.exp(log_a)
        u = jnp.sqrt(1.0 - a * a) * (i * xc)
        acum, h = _linear_scan(a, u)
        h = h + acum * h_sc[:, sl]
        h_sc[:, sl] = h[ts - 1:ts]
        gate = gate_ref[0, :, sl].astype(F32)
        o_ref[0, :, sl] = (h * _gelu(gate)).astype(o_ref.dtype)


def _lru(z, conv_w, conv_b, w_a, b_a, w_x, b_x, lru_lambda, lru_w):
    B, S, _ = z.shape
    nblocks = w_a.shape[0]
    assert w_a.shape[1] == LANES and lru_w == nblocks * LANES
    cw = _pick(lru_w, 512)
    nblk = cw // LANES
    ts = _pick(S, 512, SUBLANES)
    ncb = lru_w // cw
    row = lambda v: v.reshape(1, lru_w)
    vec_spec = pl.BlockSpec((1, cw), lambda b, c, s: (0, c))
    return pl.pallas_call(
        functools.partial(_lru_kernel, nblk=nblk),
        grid=(B, ncb, S // ts),
        in_specs=[pl.BlockSpec((1, ts, cw), lambda b, c, s: (b, s, c)),
                  pl.BlockSpec((1, ts, cw), lambda b, c, s: (b, s, c + ncb)),
                  pl.BlockSpec((conv_w.shape[0], cw), lambda b, c, s: (0, c)),
                  vec_spec,
                  pl.BlockSpec((nblk, LANES, LANES), lambda b, c, s: (c, 0, 0)),
                  vec_spec,
                  pl.BlockSpec((nblk, LANES, LANES), lambda b, c, s: (c, 0, 0)),
                  vec_spec, vec_spec],
        out_specs=pl.BlockSpec((1, ts, cw), lambda b, c, s: (b, s, c)),
        out_shape=jax.ShapeDtypeStruct((B, S, lru_w), BF16),
        scratch_shapes=[pltpu.VMEM((SUBLANES, cw), F32), pltpu.VMEM((1, cw), F32)],
        compiler_params=_params("parallel", "parallel", "arbitrary"),
        name="rg_lru",
    )(z, z, conv_w, row(conv_b), w_a.astype(BF16), row(b_a), w_x.astype(BF16), row(b_x),
      row(lru_lambda))


def _rope_kernel(pos_ref, f_ref, sg_ref, cos_ref, sin_ref):
    ang = pos_ref[0].astype(F32) * f_ref[...]
    cos_ref[0] = jnp.cos(ang)
    sin_ref[0] = jnp.sin(ang) * sg_ref[...]


def _rope_tables(positions, head_dim):
    B, S = positions.shape
    half = head_dim // 2
    inv_freq = ROPE_THETA ** (-jnp.arange(0, head_dim, 2, dtype=F32) / head_dim)
    freq_row = jnp.tile(inv_freq, LANES // half).reshape(1, LANES)
    sign_row = jnp.tile(jnp.concatenate([-jnp.ones(half, F32), jnp.ones(half, F32)]),
                        LANES // head_dim).reshape(1, LANES)
    ts = _pick(S, 1024, SUBLANES)
    tab = jax.ShapeDtypeStruct((B, S, LANES), F32)
    return pl.pallas_call(
        _rope_kernel,
        grid=(B, S // ts),
        in_specs=[pl.BlockSpec((1, ts, 1), lambda b, s: (b, s, 0)),
                  pl.BlockSpec((1, LANES), lambda b, s: (0, 0)),
                  pl.BlockSpec((1, LANES), lambda b, s: (0, 0))],
        out_specs=[pl.BlockSpec((1, ts, LANES), lambda b, s: (b, s, 0))] * 2,
        out_shape=[tab, tab],
        compiler_params=_params("parallel", "parallel"),
        name="rope_tables",
    )(positions.reshape(B, S, 1), freq_row, sign_row)


def _qk_prep_kernel(q_ref, k_ref, cos_ref, sin_ref, gq_ref, gk_ref, p_ref, qo_ref, ko_ref,
                    *, nheads, head_dim, q_scale):
    cos = cos_ref[0]
    sin = sin_ref[0]
    pmat = p_ref[...]
    lane = lax.broadcasted_iota(jnp.int32, cos.shape, 1)
    first_half = (lane % head_dim) < (head_dim // 2)

    def prep(t, g):
        ms = jnp.dot(t * t, pmat, preferred_element_type=F32, precision=HIGHEST)
        tn = t * lax.rsqrt(ms + EPS) * g
        swapped = jnp.where(first_half, pltpu.roll(tn, LANES - head_dim // 2, 1),
                            pltpu.roll(tn, head_dim // 2, 1))
        return tn * cos + swapped * sin

    for h in range(nheads):
        sl = slice(h * LANES, (h + 1) * LANES)
        qo_ref[0, :, sl] = (prep(q_ref[0, :, sl].astype(F32), gq_ref[...]) * q_scale).astype(BF16)
        ko_ref[0, :, sl] = prep(k_ref[0, :, sl].astype(F32), gk_ref[...]).astype(BF16)


def _qk_prep(z, cos_t, sin_t, q_norm_g, k_norm_g, q_col, k_col, attn_w, head_dim):
    B, S, _ = z.shape
    assert 2 * head_dim == LANES
    nheads = attn_w // LANES
    ts = _pick(S, 256, SUBLANES)
    qb, kb = q_col // attn_w, k_col // attn_w
    lane = jnp.arange(LANES)
    pmat = (lane[:, None] // head_dim == lane[None, :] // head_dim).astype(F32) / head_dim
    out = jax.ShapeDtypeStruct((B, S, attn_w), BF16)
    tab_spec = pl.BlockSpec((1, ts, LANES), lambda b, s: (b, s, 0))
    row_spec = pl.BlockSpec((1, LANES), lambda b, s: (0, 0))
    return pl.pallas_call(
        functools.partial(_qk_prep_kernel, nheads=nheads, head_dim=head_dim,
                          q_scale=head_dim ** -0.5 * LOG2_E),
        grid=(B, S // ts),
        in_specs=[pl.BlockSpec((1, ts, attn_w), lambda b, s: (b, s, qb)),
                  pl.BlockSpec((1, ts, attn_w), lambda b, s: (b, s, kb)),
                  tab_spec, tab_spec, row_spec, row_spec,
                  pl.BlockSpec((LANES, LANES), lambda b, s: (0, 0))],
        out_specs=[pl.BlockSpec((1, ts, attn_w), lambda b, s: (b, s, 0))] * 2,
        out_shape=[out, out],
        compiler_params=_params("parallel", "parallel"),
        name="qk_prep",
    )(z, z, cos_t, sin_t, jnp.tile(q_norm_g, 2).reshape(1, LANES),
      jnp.tile(k_norm_g, 2).reshape(1, LANES), pmat)


def _attn_kernel(q_ref, k_ref, v_ref, lams_ref, g_ref, o_ref, qq_sc, m_sc, acc_sc,
                 *, tq, tk, rc, head_dim, lam_init):
    qi = pl.program_id(2)
    q = q_ref[0]
    lane = lax.broadcasted_iota(jnp.int32, q.shape, 1)
    zero = jnp.zeros_like(q)
    qq_sc[0:tq] = jnp.where(lane < head_dim, q, zero)
    qq_sc[tq:2 * tq] = jnp.where(lane >= head_dim, q, zero)
    m_sc[...] = jnp.full_like(m_sc, NEG_INF)
    acc_sc[...] = jnp.zeros_like(acc_sc)
    ones = jnp.ones((tk, LANES), BF16)

    def block(ki, diagonal):
        start = pl.multiple_of(ki * tk, tk)
        k = k_ref[0, pl.ds(start, tk), :]
        v1 = jnp.concatenate([v_ref[0, pl.ds(start, tk), :], ones], axis=1)
        for c in range(2 * tq // rc):
            rows = slice(c * rc, (c + 1) * rc)
            ncol = (c * rc) % tq + rc if diagonal else tk
            s = _dot_nt(qq_sc[rows], k[:ncol])
            if diagonal:
                row = lax.broadcasted_iota(jnp.int32, s.shape, 0) + (c * rc) % tq
                col = lax.broadcasted_iota(jnp.int32, s.shape, 1)
                s = jnp.where(col <= row, s, NEG_INF)
            m_prev = m_sc[rows]
            m_new = jnp.maximum(m_prev, jnp.max(s, axis=1, keepdims=True))
            alpha = jnp.exp2(m_prev - m_new)
            p = jnp.exp2(s - jnp.concatenate([m_new] * (ncol // LANES), axis=1)).astype(BF16)
            pv = jnp.dot(p, v1[:ncol], preferred_element_type=F32)
            acc_sc[rows] = jnp.concatenate([alpha, alpha], axis=1) * acc_sc[rows] + pv
            m_sc[rows] = m_new

    def off_diagonal_pair(i, carry):
        block(2 * i, False)
        block(2 * i + 1, False)
        return carry

    lax.fori_loop(0, qi // 2, off_diagonal_pair, 0)

    @pl.when(qi % 2 == 1)
    def _():
        block(qi - 1, False)

    block(qi, True)

    acc = acc_sc[...]
    o = acc[:, :LANES] / acc[:, LANES:]
    lv = lams_ref[...]
    lam = (jnp.exp(jnp.sum(lv[0:1] * lv[1:2], axis=1, keepdims=True))
           - jnp.exp(jnp.sum(lv[2:3] * lv[3:4], axis=1, keepdims=True)) + lam_init)
    d = o[0:tq] - lam * o[tq:2 * tq]
    ms = jnp.mean(d * d, axis=-1, keepdims=True)
    y = d * lax.rsqrt(ms + EPS) * g_ref[...] * (1.0 - lam_init)
    o_ref[0] = y.astype(o_ref.dtype)


def _attention(qn, kn, z, lams, subln_g, v_col, head_dim, lam_init):
    B, S, attn_w = qn.shape
    nheads = attn_w // LANES
    tq = tk = _pick(S, ATTN_BLOCK)
    vb = v_col // LANES
    return pl.pallas_call(
        functools.partial(_attn_kernel, tq=tq, tk=tk, rc=_pick(tq, ATTN_ROW_CHUNK),
                          head_dim=head_dim, lam_init=lam_init),
        grid=(B, nheads, S // tq),
        in_specs=[pl.BlockSpec((1, tq, LANES), lambda b, h, i: (b, i, h)),
                  pl.BlockSpec((1, S, LANES), lambda b, h, i: (b, 0, h)),
                  pl.BlockSpec((1, S, LANES), lambda b, h, i: (b, 0, vb + h)),
                  pl.BlockSpec((4, head_dim), lambda b, h, i: (0, 0)),
                  pl.BlockSpec((1, LANES), lambda b, h, i: (0, 0))],
        out_specs=pl.BlockSpec((1, tq, LANES), lambda b, h, i: (b, i, h)),
        out_shape=jax.ShapeDtypeStruct((B, S, attn_w), BF16),
        scratch_shapes=[pltpu.VMEM((2 * tq, LANES), BF16), pltpu.VMEM((2 * tq, LANES), F32),
                        pltpu.VMEM((2 * tq, 2 * LANES), F32)],
        compiler_params=_params("parallel", "parallel", "parallel"),
        name="diff_attention",
    )(qn, kn, z, lams, subln_g.reshape(1, LANES))


def _merge_kernel(rec_ref, att_ref, wl_ref, wa_ref, gl_ref, ga_ref, o_ref):
    yl = jnp.dot(rec_ref[...], wl_ref[...], preferred_element_type=F32)
    ya = jnp.dot(att_ref[...], wa_ref[...], preferred_element_type=F32)
    gl = jax.nn.sigmoid(gl_ref[...].astype(F32))
    ga = jax.nn.sigmoid(ga_ref[...].astype(F32))
    o_ref[...] = (gl * yl + ga * ya).astype(o_ref.dtype)


def _merge(rec, att, w_lru_out, w_attn_out, z2, gl_col, ga_col):
    T, lru_w = rec.shape
    attn_w = att.shape[1]
    D = w_lru_out.shape[1]
    tm, tn = _pick(T, 1024), _pick(math.gcd(D, gl_col, ga_col), 512)
    glb, gab = gl_col // tn, ga_col // tn
    return pl.pallas_call(
        _merge_kernel,
        grid=(T // tm, D // tn),
        in_specs=[pl.BlockSpec((tm, lru_w), lambda i, j: (i, 0)),
                  pl.BlockSpec((tm, attn_w), lambda i, j: (i, 0)),
                  pl.BlockSpec((lru_w, tn), lambda i, j: (0, j)),
                  pl.BlockSpec((attn_w, tn), lambda i, j: (0, j)),
                  pl.BlockSpec((tm, tn), lambda i, j: (i, glb + j)),
                  pl.BlockSpec((tm, tn), lambda i, j: (i, gab + j))],
        out_specs=pl.BlockSpec((tm, tn), lambda i, j: (i, j)),
        out_shape=jax.ShapeDtypeStruct((T, D), BF16),
        compiler_params=_params("parallel", "parallel"),
        name="merge",
    )(rec, att, w_lru_out, w_attn_out, z2, z2)


def _proj_res_kernel(a_ref, w_ref, x_ref, g_ref, o_ref):
    y = jnp.dot(a_ref[...], w_ref[...], preferred_element_type=F32)
    o_ref[...] = x_ref[...] + g_ref[0] * y


def _proj_residual(a, w, x2, gate, seq):
    T, K = a.shape
    D = w.shape[1]
    tm, tn = _pick(math.gcd(T, seq), 1024), _pick(D, 512)
    per_b = seq // tm
    return pl.pallas_call(
        _proj_res_kernel,
        grid=(T // tm, D // tn),
        in_specs=[pl.BlockSpec((tm, K), lambda i, j: (i, 0)),
                  pl.BlockSpec((K, tn), lambda i, j: (0, j)),
                  pl.BlockSpec((tm, tn), lambda i, j: (i, j)),
                  pl.BlockSpec((1, 1, tn), lambda i, j: (i // per_b, 0, j))],
        out_specs=pl.BlockSpec((tm, tn), lambda i, j: (i, j)),
        out_shape=jax.ShapeDtypeStruct((T, D), F32),
        compiler_params=_params("parallel", "parallel"),
        name="proj_residual",
    )(a, w, x2, gate)


def _topk_rows(s, extras, k):
    nrows = s.shape[0]
    rowf = lax.broadcasted_iota(jnp.int32, s.shape, 0).astype(F32)
    vals, picked = [], [[] for _ in extras]
    idxs = []
    rank = jnp.full(s.shape, float(k), F32)
    for r in range(k):
        m = jnp.max(s, axis=0, keepdims=True)
        idx = jnp.min(jnp.where(s == m, rowf, float(nrows)), axis=0, keepdims=True)
        sel = rowf == idx
        vals.append(m)
        idxs.append(idx)
        for lst, arr in zip(picked, extras):
            lst.append(jnp.max(jnp.where(sel, arr, -1.0), axis=0, keepdims=True))
        s = jnp.where(sel, -jnp.inf, s)
        rank = jnp.where(sel, float(r), rank)
    return vals, idxs, picked, rank


def _stack_rows(rows):
    k, n = len(rows), rows[0].shape[1]
    row = lax.broadcasted_iota(jnp.int32, (k, n), 0)
    out = jnp.zeros((k, n), F32)
    for r, v in enumerate(rows):
        out = jnp.where(row == r, v, out)
    return out


def _topk_kernel(q_ref, keys_ref, kap_ref, cnt_ref, e2_ref, rk2_ref, *, k):
    q = q_ref[...]
    half = q.shape[1] // 2
    s1 = _dot_nt(keys_ref[0], q[:, :half], HIGHEST)
    s2 = _dot_nt(keys_ref[1], q[:, half:], HIGHEST)
    v1, i1, _, _ = _topk_rows(s1, (), k)
    v2, _, _, rk2 = _topk_rows(s2, (), k)
    v1s, i1s = _stack_rows(v1), _stack_rows(i1)
    v2s = _stack_rows(v2)
    n = q.shape[0]
    vals, cas = [], []
    for i in range(k // 2):
        cnt = k // (i + 1)
        rows = -(-cnt // SUBLANES) * SUBLANES
        grp = v1[i] + v2s[0:rows]
        if cnt < rows:
            jrow = lax.broadcasted_iota(jnp.int32, grp.shape, 0)
            grp = jnp.where(jrow < cnt, grp, -jnp.inf)
        vals.append(grp)
        cas.append(jnp.broadcast_to(i1[i], (rows, n)))
    vals.append(v1s[k // 2:k] + v2[0])
    cas.append(i1s[k // 2:k])
    best, _, (a_sel,), _ = _topk_rows(jnp.concatenate(vals, axis=0),
                                      (jnp.concatenate(cas, axis=0),), k)
    z = jnp.sum(jnp.exp(_stack_rows(best) - best[0]), axis=0, keepdims=True)
    keyf = lax.broadcasted_iota(jnp.int32, s1.shape, 0).astype(F32)
    count = jnp.zeros(s1.shape, F32)
    for a in a_sel:
        count = count + jnp.where(keyf == a, 1.0, 0.0)
    kap_ref[...] = jnp.exp(s1 - v1[0]) / z
    cnt_ref[...] = count
    e2_ref[...] = jnp.exp(s2 - v2[0]).astype(e2_ref.dtype)
    rk2_ref[...] = rk2.astype(rk2_ref.dtype)


def _peer_topk(q, sub_keys, nheads):
    T = q.shape[0]
    nkeys, half = sub_keys.shape[1], sub_keys.shape[2]
    assert half == LANES and nkeys == LANES
    k = PEER_TOPK
    assert k % (2 * SUBLANES) == 0
    tt = _pick(T, PEER_TOPK_TOKENS)
    out = jax.ShapeDtypeStruct((nheads * nkeys, T), F32)
    out_b = jax.ShapeDtypeStruct((nheads * nkeys, T), BF16)
    out_spec = pl.BlockSpec((nkeys, tt), lambda i, h: (h, i))
    return pl.pallas_call(
        functools.partial(_topk_kernel, k=k),
        grid=(T // tt, nheads),
        in_specs=[pl.BlockSpec((tt, 2 * half), lambda i, h: (i, h)),
                  pl.BlockSpec((2, nkeys, half), lambda i, h: (0, 0, 0))],
        out_specs=[out_spec] * 4,
        out_shape=[out, out, out_b, out_b],
        compiler_params=_params("parallel", "parallel"),
        name="peer_topk",
    )(q, sub_keys)


def _peer_kernel(h_ref, u_ref, vt_ref, kap_ref, cnt_ref, e2_ref, rk2_ref, o_ref, w_sc,
                 *, nheads, nkeys):
    j = pl.program_id(1)
    nj = pl.num_programs(1) - 1
    ka = u_ref.shape[0] // nkeys

    def weights():
        parts = []
        for aa in range(ka):
            a = j * ka + aa
            g = None
            for h in range(nheads):
                kap = kap_ref[pl.ds(h * nkeys + a, 1), :].astype(BF16)
                cnt = cnt_ref[pl.ds(h * nkeys + a, 1), :].astype(BF16)
                e2 = e2_ref[h * nkeys:(h + 1) * nkeys, :]
                rk2 = rk2_ref[h * nkeys:(h + 1) * nkeys, :]
                term = jnp.where(rk2 < cnt, e2, jnp.zeros_like(e2)) * kap
                g = term if g is None else g + term
            parts.append(g)
        gates = jnp.concatenate(parts, axis=0)
        s = _dot_nt(u_ref[...], h_ref[...])
        return _gelu(s).astype(BF16) * gates

    cur = j % 2

    @pl.when(j == 0)
    def _():
        o_ref[...] = jnp.zeros_like(o_ref)
        w_sc[0] = weights()

    @pl.when(jnp.logical_and(j > 0, j < nj))
    def _():
        w_next = weights()
        o_ref[...] += jnp.dot(vt_ref[...], w_sc[1 - cur], preferred_element_type=F32)
        w_sc[cur] = w_next

    @pl.when(j == nj)
    def _():
        o_ref[...] += jnp.dot(vt_ref[...], w_sc[1 - cur], preferred_element_type=F32)


def _peer_tiles(T, E):
    return _pick(T, 512), _pick(E, 512)


def _peer_dense(h2, u_tab, vt_tiles, kap, cnt, e2, rk2, nheads, nkeys):
    T, D = h2.shape
    E = u_tab.shape[0]
    tm, te = _peer_tiles(T, E)
    assert te % nkeys == 0 and vt_tiles.shape == (E // te, D, te)
    once = pl.Buffered(1)
    tab_spec = pl.BlockSpec((nheads * nkeys, tm), lambda i, j: (0, i), pipeline_mode=once)
    nj = E // te
    return pl.pallas_call(
        functools.partial(_peer_kernel, nheads=nheads, nkeys=nkeys),
        grid=(T // tm, nj + 1),
        in_specs=[pl.BlockSpec((tm, D), lambda i, j: (i, 0), pipeline_mode=once),
                  pl.BlockSpec((te, D), lambda i, j: (jnp.minimum(j, nj - 1), 0)),
                  pl.BlockSpec((None, D, te), lambda i, j: (jnp.maximum(j - 1, 0), 0, 0)),
                  tab_spec, tab_spec, tab_spec, tab_spec],
        out_specs=pl.BlockSpec((D, tm), lambda i, j: (0, i)),
        out_shape=jax.ShapeDtypeStruct((D, T), F32),
        scratch_shapes=[pltpu.VMEM((2, te, tm), BF16)],
        compiler_params=_params("parallel", "arbitrary"),
        name="peer_dense",
    )(h2, u_tab, vt_tiles, kap, cnt, e2, rk2)


def _residual_kernel(x_ref, yt_ref, g_ref, o_ref):
    o_ref[0] = x_ref[0] + g_ref[0] * yt_ref[...].T


def _residual(x, y_t, gate):
    B, S, D = x.shape
    ts = _pick(S, 256)
    per_b = S // ts
    blk = pl.BlockSpec((1, ts, D), lambda b, s: (b, s, 0))
    return pl.pallas_call(
        _residual_kernel,
        grid=(B, per_b),
        in_specs=[blk, pl.BlockSpec((D, ts), lambda b, s: (0, b * per_b + s)),
                  pl.BlockSpec((1, 1, D), lambda b, s: (b, 0, 0))],
        out_specs=blk,
        out_shape=jax.ShapeDtypeStruct((B, S, D), F32),
        compiler_params=_params("parallel", "parallel"),
        name="residual",
    )(x, y_t, gate)


def kernel(x, c, positions, norm1_g, norm2_g, w_ada, b_ada, w_in, conv_w, conv_b, w_a, b_a, w_x,
           b_x, lru_lambda, q_norm_g, k_norm_g, lambda_q1, lambda_k1, lambda_q2, lambda_k2,
           subln_g, w_lru_out, w_attn_out, w_o, w_pq, sub_keys, u_tab, v_tab):
    B, S, D = x.shape
    T = B * S
    depth = w_in.shape[0]
    lru_w = conv_w.shape[-1]
    attn_w = w_attn_out.shape[1]
    head_dim = q_norm_g.shape[-1]
    nkeys = sub_keys.shape[2]
    peer_heads = w_pq.shape[2] // (2 * sub_keys.shape[3])
    assert B <= SUBLANES and subln_g.shape[-1] == LANES
    q_col = 2 * lru_w
    k_col = q_col + attn_w
    v_col = k_col + attn_w
    gl_col = v_col + attn_w
    ga_col = gl_col + D

    c_pad = jnp.pad(c, ((0, SUBLANES - B), (0, 0)))
    mod = _ada(c_pad, w_ada, b_ada)[:, :B].reshape(depth, B, N_MOD, 1, D)
    cos_t, sin_t = _rope_tables(positions, head_dim)

    for l in range(depth):
        lam_init = 0.8 - 0.6 * math.exp(-0.3 * l)
        shift1, scale1, gate1 = mod[l, :, 0], mod[l, :, 1], mod[l, :, 2]
        shift2, scale2, gate2 = mod[l, :, 3], mod[l, :, 4], mod[l, :, 5]

        h = _norm_mod(x, norm1_g[l], scale1, shift1)
        z2 = _matmul(h.reshape(T, D), w_in[l].astype(BF16), BF16, "in_proj")
        z = z2.reshape(B, S, -1)
        rec = _lru(z, conv_w[l], conv_b[l], w_a[l], b_a[l], w_x[l], b_x[l], lru_lambda[l], lru_w)
        qn, kn = _qk_prep(z, cos_t, sin_t, q_norm_g[l], k_norm_g[l], q_col, k_col, attn_w, head_dim)
        lams = jnp.stack([lambda_q1[l], lambda_k1[l], lambda_q2[l], lambda_k2[l]])
        att = _attention(qn, kn, z, lams, subln_g[l], v_col, head_dim, lam_init)
        merged = _merge(rec.reshape(T, lru_w), att.reshape(T, attn_w), w_lru_out[l].astype(BF16),
                        w_attn_out[l].astype(BF16), z2, gl_col, ga_col)
        x = _proj_residual(merged, w_o[l].astype(BF16), x.reshape(T, D), gate1, S).reshape(B, S, D)

        h2 = _norm_mod(x, norm2_g[l], scale2, shift2).reshape(T, D)
        pq = _matmul(h2, w_pq[l].astype(BF16), F32, "peer_query")
        kap, cnt, e2, rk2 = _peer_topk(pq, sub_keys[l], peer_heads)
        te = _peer_tiles(T, u_tab.shape[1])[1]
        vt_tiles = v_tab[l].astype(BF16).reshape(-1, te, D).transpose(0, 2, 1)
        y_t = _peer_dense(h2, u_tab[l].astype(BF16), vt_tiles, kap, cnt, e2, rk2,
                          peer_heads, nkeys)
        x = _residual(x, y_t, gate2)
    return x
```

```python
import functools
import math

import jax
import jax.numpy as jnp
from jax import lax
from jax.experimental import pallas as pl
from jax.experimental.pallas import tpu as pltpu

F32 = jnp.float32
BF16 = jnp.bfloat16

LANES = 128
SUBLANES = 8
VMEM_LIMIT = 56 * 1024 * 1024
EPS = 1e-6
NEG_INF = -1e30
LRU_C = 8.0
ROPE_THETA = 10000.0
PEER_TOPK = 16
N_MOD = 6
INV_SQRT2 = 0.7071067811865476
LOG2_E = 1.4426950408889634
ATTN_BLOCK = 1024
ATTN_ROW_CHUNK = 512
PEER_TOPK_TOKENS = 512
HIGHEST = lax.Precision.HIGHEST


def _pick(n, pref, align=LANES):
    if n <= pref:
        return n
    t = (pref // align) * align
    while t > align and n % t:
        t -= align
    assert n % t == 0, (n, pref, align)
    return t


def _params(*sem):
    return pltpu.CompilerParams(dimension_semantics=sem, vmem_limit_bytes=VMEM_LIMIT)


def _dot_nt(a, b, precision=None):
    return lax.dot_general(a, b, (((1,), (1,)), ((), ())), precision=precision,
                           preferred_element_type=F32)


def _gelu(x):
    return 0.5 * x * (1.0 + lax.erf(x * INV_SQRT2))


def _ada_kernel(c_ref, w_ref, b_ref, o_ref):
    c = c_ref[...]
    ca = c * jax.nn.sigmoid(c)
    o_ref[0] = jnp.dot(ca, w_ref[0], preferred_element_type=F32, precision=HIGHEST) + b_ref[0]


def _ada(c_pad, w_ada, b_ada):
    L, D, N = w_ada.shape
    tn = _pick(N, 512)
    return pl.pallas_call(
        _ada_kernel,
        grid=(L, N // tn),
        in_specs=[pl.BlockSpec((SUBLANES, D), lambda l, j: (0, 0)),
                  pl.BlockSpec((1, D, tn), lambda l, j: (l, 0, j)),
                  pl.BlockSpec((1, 1, tn), lambda l, j: (l, 0, j))],
        out_specs=pl.BlockSpec((1, SUBLANES, tn), lambda l, j: (l, 0, j)),
        out_shape=jax.ShapeDtypeStruct((L, SUBLANES, N), F32),
        compiler_params=_params("parallel", "parallel"),
        name="adaln",
    )(c_pad, w_ada, b_ada.reshape(L, 1, N))


def _norm_mod_kernel(x_ref, g_ref, sc_ref, sh_ref, o_ref):
    x = x_ref[0]
    ms = jnp.mean(x * x, axis=-1, keepdims=True)
    y = x * lax.rsqrt(ms + EPS) * g_ref[...]
    o_ref[0] = (y * (1.0 + sc_ref[0]) + sh_ref[0]).astype(o_ref.dtype)


def _norm_mod(x, g, scale, shift):
    B, S, D = x.shape
    ts = _pick(S, 256)
    return pl.pallas_call(
        _norm_mod_kernel,
        grid=(B, S // ts),
        in_specs=[pl.BlockSpec((1, ts, D), lambda b, s: (b, s, 0)),
                  pl.BlockSpec((1, D), lambda b, s: (0, 0)),
                  pl.BlockSpec((1, 1, D), lambda b, s: (b, 0, 0)),
                  pl.BlockSpec((1, 1, D), lambda b, s: (b, 0, 0))],
        out_specs=pl.BlockSpec((1, ts, D), lambda b, s: (b, s, 0)),
        out_shape=jax.ShapeDtypeStruct((B, S, D), BF16),
        compiler_params=_params("parallel", "parallel"),
        name="norm_mod",
    )(x, g.reshape(1, D), scale, shift)


def _mm_kernel(a_ref, b_ref, o_ref):
    o_ref[...] = jnp.dot(a_ref[...], b_ref[...], preferred_element_type=F32).astype(o_ref.dtype)


def _matmul(a, b, out_dtype, name):
    M, K = a.shape
    _, N = b.shape
    tm, tn = _pick(M, 1024), _pick(N, 1024)
    return pl.pallas_call(
        _mm_kernel,
        grid=(M // tm, N // tn),
        in_specs=[pl.BlockSpec((tm, K), lambda i, j: (i, 0)),
                  pl.BlockSpec((K, tn), lambda i, j: (0, j))],
        out_specs=pl.BlockSpec((tm, tn), lambda i, j: (i, j)),
        out_shape=jax.ShapeDtypeStruct((M, N), out_dtype),
        compiler_params=_params("parallel", "parallel"),
        name=name,
    )(a, b)


def _shift_rows(x, s, fill):
    row = lax.broadcasted_iota(jnp.int32, x.shape, 0)
    return jnp.where(row >= s, pltpu.roll(x, s, 0), fill)


def _linear_scan(a, u):
    n = a.shape[0]
    s = 1
    while s < n:
        u = a * _shift_rows(u, s, 0.0) + u
        a = a * _shift_rows(a, s, 1.0)
        s *= 2
    return a, u


def _lru_kernel(x_ref, gate_ref, cw_ref, cb_ref, wa_ref, ba_ref, wx_ref, bx_ref, lam_ref,
                o_ref, tail_sc, h_sc, *, nblk):
    @pl.when(pl.program_id(2) == 0)
    def _():
        tail_sc[...] = jnp.zeros_like(tail_sc)
        h_sc[...] = jnp.zeros_like(h_sc)

    ts = x_ref.shape[1]
    row8 = lax.broadcasted_iota(jnp.int32, (SUBLANES, LANES), 0)
    for j in range(nblk):
        sl = slice(j * LANES, (j + 1) * LANES)
        x = x_ref[0, :, sl].astype(F32)
        tail = tail_sc[:, sl]
        cw = cw_ref[:, sl]
        nk = cw.shape[0]
        xc = cb_ref[:, sl] + cw[nk - 1:nk] * x
        for d in range(1, nk):
            xr = pltpu.roll(x, d, 0)
            head = jnp.where(row8 < d, pltpu.roll(tail, d, 0), xr[:SUBLANES])
            xd = jnp.concatenate([head, xr[SUBLANES:]], axis=0)
            xc = xc + cw[nk - 1 - d:nk - d] * xd
        tail_sc[:, sl] = x[ts - SUBLANES:]

        xcb = xc.astype(BF16)
        r = jax.nn.sigmoid(jnp.dot(xcb, wa_ref[j], preferred_element_type=F32) + ba_ref[:, sl])
        i = jax.nn.sigmoid(jnp.dot(xcb, wx_ref[j], preferred_element_type=F32) + bx_ref[:, sl])
        neg_lam = -lam_ref[:, sl]
        softplus = jnp.maximum(neg_lam, 0.0) + jnp.log1p(jnp.exp(-jnp.abs(neg_lam)))
        log_a = (-LRU_C) * r * softplus
        a = jnp.exp(log_a)
        u = jnp.sqrt(1.0 - a * a) * (i * xc)
        acum, h = _linear_scan(a, u)
        h = h + acum * h_sc[:, sl]
        h_sc[:, sl] = h[ts - 1:ts]
        gate = gate_ref[0, :, sl].astype(F32)
        o_ref[0, :, sl] = (h * _gelu(gate)).astype(o_ref.dtype)


def _lru(z, conv_w, conv_b, w_a, b_a, w_x, b_x, lru_lambda, lru_w):
    B, S, _ = z.shape
    nblocks = w_a.shape[0]
    assert w_a.shape[1] == LANES and lru_w == nblocks * LANES
    cw = _pick(lru_w, 512)
    nblk = cw // LANES
    ts = _pick(S, 512, SUBLANES)
    ncb = lru_w // cw
    row = lambda v: v.reshape(1, lru_w)
    vec_spec = pl.BlockSpec((1, cw), lambda b, c, s: (0, c))
    return pl.pallas_call(
        functools.partial(_lru_kernel, nblk=nblk),
        grid=(B, ncb, S // ts),
        in_specs=[pl.BlockSpec((1, ts, cw), lambda b, c, s: (b, s, c)),
                  pl.BlockSpec((1, ts, cw), lambda b, c, s: (b, s, c + ncb)),
                  pl.BlockSpec((conv_w.shape[0], cw), lambda b, c, s: (0, c)),
                  vec_spec,
                  pl.BlockSpec((nblk, LANES, LANES), lambda b, c, s: (c, 0, 0)),
                  vec_spec,
                  pl.BlockSpec((nblk, LANES, LANES), lambda b, c, s: (c, 0, 0)),
                  vec_spec, vec_spec],
        out_specs=pl.BlockSpec((1, ts, cw), lambda b, c, s: (b, s, c)),
        out_shape=jax.ShapeDtypeStruct((B, S, lru_w), BF16),
        scratch_shapes=[pltpu.VMEM((SUBLANES, cw), F32), pltpu.VMEM((1, cw), F32)],
        compiler_params=_params("parallel", "parallel", "arbitrary"),
        name="rg_lru",
    )(z, z, conv_w, row(conv_b), w_a.astype(BF16), row(b_a), w_x.astype(BF16), row(b_x),
      row(lru_lambda))


def _rope_kernel(pos_ref, f_ref, sg_ref, cos_ref, sin_ref):
    ang = pos_ref[0].astype(F32) * f_ref[...]
    cos_ref[0] = jnp.cos(ang)
    sin_ref[0] = jnp.sin(ang) * sg_ref[...]


def _rope_tables(positions, head_dim):
    B, S = positions.shape
    half = head_dim // 2
    inv_freq = ROPE_THETA ** (-jnp.arange(0, head_dim, 2, dtype=F32) / head_dim)
    freq_row = jnp.tile(inv_freq, LANES // half).reshape(1, LANES)
    sign_row = jnp.tile(jnp.concatenate([-jnp.ones(half, F32), jnp.ones(half, F32)]),
                        LANES // head_dim).reshape(1, LANES)
    ts = _pick(S, 1024, SUBLANES)
    tab = jax.ShapeDtypeStruct((B, S, LANES), F32)
    return pl.pallas_call(
        _rope_kernel,
        grid=(B, S // ts),
        in_specs=[pl.BlockSpec((1, ts, 1), lambda b, s: (b, s, 0)),
                  pl.BlockSpec((1, LANES), lambda b, s: (0, 0)),
                  pl.BlockSpec((1, LANES), lambda b, s: (0, 0))],
        out_specs=[pl.BlockSpec((1, ts, LANES), lambda b, s: (b, s, 0))] * 2,
        out_shape=[tab, tab],
        compiler_params=_params("parallel", "parallel"),
        name="rope_tables",
    )(positions.reshape(B, S, 1), freq_row, sign_row)


def _qk_prep_kernel(q_ref, k_ref, cos_ref, sin_ref, gq_ref, gk_ref, p_ref, qo_ref, ko_ref,
                    *, nheads, head_dim, q_scale):
    cos = cos_ref[0]
    sin = sin_ref[0]
    pmat = p_ref[...]
    lane = lax.broadcasted_iota(jnp.int32, cos.shape, 1)
    first_half = (lane % head_dim) < (head_dim // 2)

    def prep(t, g):
        ms = jnp.dot(t * t, pmat, preferred_element_type=F32, precision=HIGHEST)
        tn = t * lax.rsqrt(ms + EPS) * g
        swapped = jnp.where(first_half, pltpu.roll(tn, LANES - head_dim // 2, 1),
                            pltpu.roll(tn, head_dim // 2, 1))
        return tn * cos + swapped * sin

    for h in range(nheads):
        sl = slice(h * LANES, (h + 1) * LANES)
        qo_ref[0, :, sl] = (prep(q_ref[0, :, sl].astype(F32), gq_ref[...]) * q_scale).astype(BF16)
        ko_ref[0, :, sl] = prep(k_ref[0, :, sl].astype(F32), gk_ref[...]).astype(BF16)


def _qk_prep(z, cos_t, sin_t, q_norm_g, k_norm_g, q_col, k_col, attn_w, head_dim):
    B, S, _ = z.shape
    assert 2 * head_dim == LANES
    nheads = attn_w // LANES
    ts = _pick(S, 256, SUBLANES)
    qb, kb = q_col // attn_w, k_col // attn_w
    lane = jnp.arange(LANES)
    pmat = (lane[:, None] // head_dim == lane[None, :] // head_dim).astype(F32) / head_dim
    out = jax.ShapeDtypeStruct((B, S, attn_w), BF16)
    tab_spec = pl.BlockSpec((1, ts, LANES), lambda b, s: (b, s, 0))
    row_spec = pl.BlockSpec((1, LANES), lambda b, s: (0, 0))
    return pl.pallas_call(
        functools.partial(_qk_prep_kernel, nheads=nheads, head_dim=head_dim,
                          q_scale=head_dim ** -0.5 * LOG2_E),
        grid=(B, S // ts),
        in_specs=[pl.BlockSpec((1, ts, attn_w), lambda b, s: (b, s, qb)),
                  pl.BlockSpec((1, ts, attn_w), lambda b, s: (b, s, kb)),
                  tab_spec, tab_spec, row_spec, row_spec,
                  pl.BlockSpec((LANES, LANES), lambda b, s: (0, 0))],
        out_specs=[pl.BlockSpec((1, ts, attn_w), lambda b, s: (b, s, 0))] * 2,
        out_shape=[out, out],
        compiler_params=_params("parallel", "parallel"),
        name="qk_prep",
    )(z, z, cos_t, sin_t, jnp.tile(q_norm_g, 2).reshape(1, LANES),
      jnp.tile(k_norm_g, 2).reshape(1, LANES), pmat)


def _attn_kernel(q_ref, k_ref, v_ref, lams_ref, g_ref, o_ref, qq_sc, m_sc, acc_sc,
                 *, tq, tk, rc, head_dim, lam_init):
    qi = pl.program_id(2)
    q = q_ref[0]
    lane = lax.broadcasted_iota(jnp.int32, q.shape, 1)
    zero = jnp.zeros_like(q)
    qq_sc[0:tq] = jnp.where(lane < head_dim, q, zero)
    qq_sc[tq:2 * tq] = jnp.where(lane >= head_dim, q, zero)
    m_sc[...] = jnp.full_like(m_sc, NEG_INF)
    acc_sc[...] = jnp.zeros_like(acc_sc)
    ones = jnp.ones((tk, LANES), BF16)

    def block(ki, diagonal):
        start = pl.multiple_of(ki * tk, tk)
        k = k_ref[0, pl.ds(start, tk), :]
        v1 = jnp.concatenate([v_ref[0, pl.ds(start, tk), :], ones], axis=1)
        for c in range(2 * tq // rc):
            rows = slice(c * rc, (c + 1) * rc)
            ncol = (c * rc) % tq + rc if diagonal else tk
            s = _dot_nt(qq_sc[rows], k[:ncol])
            if diagonal:
                row = lax.broadcasted_iota(jnp.int32, s.shape, 0) + (c * rc) % tq
                col = lax.broadcasted_iota(jnp.int32, s.shape, 1)
                s = jnp.where(col <= row, s, NEG_INF)
            m_prev = m_sc[rows]
            m_new = jnp.maximum(m_prev, jnp.max(s, axis=1, keepdims=True))
            alpha = jnp.exp2(m_prev - m_new)
            p = jnp.exp2(s - jnp.concatenate([m_new] * (ncol // LANES), axis=1)).astype(BF16)
            pv = jnp.dot(p, v1[:ncol], preferred_element_type=F32)
            acc_sc[rows] = jnp.concatenate([alpha, alpha], axis=1) * acc_sc[rows] + pv
            m_sc[rows] = m_new

    def off_diagonal_pair(i, carry):
        block(2 * i, False)
        block(2 * i + 1, False)
        return carry

    lax.fori_loop(0, qi // 2, off_diagonal_pair, 0)

    @pl.when(qi % 2 == 1)
    def _():
        block(qi - 1, False)

    block(qi, True)

    acc = acc_sc[...]
    o = acc[:, :LANES] / acc[:, LANES:]
    lv = lams_ref[...]
    lam = (jnp.exp(jnp.sum(lv[0:1] * lv[1:2], axis=1, keepdims=True))
           - jnp.exp(jnp.sum(lv[2:3] * lv[3:4], axis=1, keepdims=True)) + lam_init)
    d = o[0:tq] - lam * o[tq:2 * tq]
    ms = jnp.mean(d * d, axis=-1, keepdims=True)
    y = d * lax.rsqrt(ms + EPS) * g_ref[...] * (1.0 - lam_init)
    o_ref[0] = y.astype(o_ref.dtype)


def _attention(qn, kn, z, lams, subln_g, v_col, head_dim, lam_init):
    B, S, attn_w = qn.shape
    nheads = attn_w // LANES
    tq = tk = _pick(S, ATTN_BLOCK)
    vb = v_col // LANES
    return pl.pallas_call(
        functools.partial(_attn_kernel, tq=tq, tk=tk, rc=_pick(tq, ATTN_ROW_CHUNK),
                          head_dim=head_dim, lam_init=lam_init),
        grid=(B, nheads, S // tq),
        in_specs=[pl.BlockSpec((1, tq, LANES), lambda b, h, i: (b, i, h)),
                  pl.BlockSpec((1, S, LANES), lambda b, h, i: (b, 0, h)),
                  pl.BlockSpec((1, S, LANES), lambda b, h, i: (b, 0, vb + h)),
                  pl.BlockSpec((4, head_dim), lambda b, h, i: (0, 0)),
                  pl.BlockSpec((1, LANES), lambda b, h, i: (0, 0))],
        out_specs=pl.BlockSpec((1, tq, LANES), lambda b, h, i: (b, i, h)),
        out_shape=jax.ShapeDtypeStruct((B, S, attn_w), BF16),
        scratch_shapes=[pltpu.VMEM((2 * tq, LANES), BF16), pltpu.VMEM((2 * tq, LANES), F32),
                        pltpu.VMEM((2 * tq, 2 * LANES), F32)],
        compiler_params=_params("parallel", "parallel", "parallel"),
        name="diff_attention",
    )(qn, kn, z, lams, subln_g.reshape(1, LANES))


def _merge_kernel(rec_ref, att_ref, wl_ref, wa_ref, gl_ref, ga_ref, o_ref):
    yl = jnp.dot(rec_ref[...], wl_ref[...], preferred_element_type=F32)
    ya = jnp.dot(att_ref[...], wa_ref[...], preferred_element_type=F32)
    gl = jax.nn.sigmoid(gl_ref[...].astype(F32))
    ga = jax.nn.sigmoid(ga_ref[...].astype(F32))
    o_ref[...] = (gl * yl + ga * ya).astype(o_ref.dtype)


def _merge(rec, att, w_lru_out, w_attn_out, z2, gl_col, ga_col):
    T, lru_w = rec.shape
    attn_w = att.shape[1]
    D = w_lru_out.shape[1]
    tm, tn = _pick(T, 1024), _pick(math.gcd(D, gl_col, ga_col), 512)
    glb, gab = gl_col // tn, ga_col // tn
    return pl.pallas_call(
        _merge_kernel,
        grid=(T // tm, D // tn),
        in_specs=[pl.BlockSpec((tm, lru_w), lambda i, j: (i, 0)),
                  pl.BlockSpec((tm, attn_w), lambda i, j: (i, 0)),
                  pl.BlockSpec((lru_w, tn), lambda i, j: (0, j)),
                  pl.BlockSpec((attn_w, tn), lambda i, j: (0, j)),
                  pl.BlockSpec((tm, tn), lambda i, j: (i, glb + j)),
                  pl.BlockSpec((tm, tn), lambda i, j: (i, gab + j))],
        out_specs=pl.BlockSpec((tm, tn), lambda i, j: (i, j)),
        out_shape=jax.ShapeDtypeStruct((T, D), BF16),
        compiler_params=_params("parallel", "parallel"),
        name="merge",
    )(rec, att, w_lru_out, w_attn_out, z2, z2)


def _proj_res_kernel(a_ref, w_ref, x_ref, g_ref, o_ref):
    y = jnp.dot(a_ref[...], w_ref[...], preferred_element_type=F32)
    o_ref[...] = x_ref[...] + g_ref[0] * y


def _proj_residual(a, w, x2, gate, seq):
    T, K = a.shape
    D = w.shape[1]
    tm, tn = _pick(math.gcd(T, seq), 1024), _pick(D, 512)
    per_b = seq // tm
    return pl.pallas_call(
        _proj_res_kernel,
        grid=(T // tm, D // tn),
        in_specs=[pl.BlockSpec((tm, K), lambda i, j: (i, 0)),
                  pl.BlockSpec((K, tn), lambda i, j: (0, j)),
                  pl.BlockSpec((tm, tn), lambda i, j: (i, j)),
                  pl.BlockSpec((1, 1, tn), lambda i, j: (i // per_b, 0, j))],
        out_specs=pl.BlockSpec((tm, tn), lambda i, j: (i, j)),
        out_shape=jax.ShapeDtypeStruct((T, D), F32),
        compiler_params=_params("parallel", "parallel"),
        name="proj_residual",
    )(a, w, x2, gate)


def _topk_rows(s, extras, k):
    nrows = s.shape[0]
    rowf = lax.broadcasted_iota(jnp.int32, s.shape, 0).astype(F32)
    vals, picked = [], [[] for _ in extras]
    idxs = []
    rank = jnp.full(s.shape, float(k), F32)
    for r in range(k):
        m = jnp.max(s, axis=0, keepdims=True)
        idx = jnp.min(jnp.where(s == m, rowf, float(nrows)), axis=0, keepdims=True)
        sel = rowf == idx
        vals.append(m)
        idxs.append(idx)
        for lst, arr in zip(picked, extras):
            lst.append(jnp.max(jnp.where(sel, arr, -1.0), axis=0, keepdims=True))
        s = jnp.where(sel, -jnp.inf, s)
        rank = jnp.where(sel, float(r), rank)
    return vals, idxs, picked, rank


def _stack_rows(rows):
    k, n = len(rows), rows[0].shape[1]
    row = lax.broadcasted_iota(jnp.int32, (k, n), 0)
    out = jnp.zeros((k, n), F32)
    for r, v in enumerate(rows):
        out = jnp.where(row == r, v, out)
    return out


def _topk_kernel(q_ref, keys_ref, kap_ref, cnt_ref, e2_ref, rk2_ref, *, k):
    q = q_ref[...]
    half = q.shape[1] // 2
    s1 = _dot_nt(keys_ref[0], q[:, :half], HIGHEST)
    s2 = _dot_nt(keys_ref[1], q[:, half:], HIGHEST)
    def write(kap, count, e2, rk2):
        kap_ref[...] = kap
        cnt_ref[...] = count
        e2_ref[...] = e2.astype(e2_ref.dtype)
        rk2_ref[...] = rk2.astype(rk2_ref.dtype)

    *tables, distinct = _select_distinct(s1, s2, k)

    @pl.when(distinct)
    def _():
        write(*tables)

    @pl.when(jnp.logical_not(distinct))
    def _():
        write(*_select_exact(s1, s2, k))


def _candidate_groups(v1, v1s, v2, v2s, k):
    groups = []
    for i in range(k // 2):
        cnt = k // (i + 1)
        rows = -(-cnt // SUBLANES) * SUBLANES
        grp = v1[i] + v2s[0:rows]
        if cnt < rows:
            jrow = lax.broadcasted_iota(jnp.int32, grp.shape, 0)
            grp = jnp.where(jrow < cnt, grp, -jnp.inf)
        groups.append(grp)
    groups.append(v1s[k // 2:k] + v2[0])
    return groups


def _select_exact(s1, s2, k):
    n = s1.shape[1]
    v1, i1, _, _ = _topk_rows(s1, (), k)
    v2, _, _, rk2 = _topk_rows(s2, (), k)
    v1s, i1s = _stack_rows(v1), _stack_rows(i1)
    groups = _candidate_groups(v1, v1s, v2, _stack_rows(v2), k)
    cas = [jnp.broadcast_to(i1[i], (g.shape[0], n)) for i, g in enumerate(groups[:-1])]
    cas.append(i1s[k // 2:k])
    best, _, (a_sel,), _ = _topk_rows(jnp.concatenate(groups, axis=0),
                                      (jnp.concatenate(cas, axis=0),), k)
    z = jnp.sum(jnp.exp(_stack_rows(best) - best[0]), axis=0, keepdims=True)
    keyf = lax.broadcasted_iota(jnp.int32, s1.shape, 0).astype(F32)
    count = jnp.zeros(s1.shape, F32)
    for a in a_sel:
        count = count + jnp.where(keyf == a, 1.0, 0.0)
    return jnp.exp(s1 - v1[0]) / z, count, jnp.exp(s2 - v2[0]), rk2


def _topk_distinct(s, k):
    vals = []
    rank = jnp.full(s.shape, float(k), F32)
    for r in range(k):
        m = jnp.max(s, axis=0, keepdims=True)
        sel = s == m
        vals.append(m)
        rank = jnp.where(sel, float(r), rank)
        s = jnp.where(sel, -jnp.inf, s)
    dropped = jnp.sum(jnp.where(rank < k, 1.0, 0.0), axis=0, keepdims=True)
    return vals, rank, dropped


def _select_distinct(s1, s2, k):
    v1, rank1, d1 = _topk_distinct(s1, k)
    v2, rank2, d2 = _topk_distinct(s2, k)
    groups = _candidate_groups(v1, _stack_rows(v1), v2, _stack_rows(v2), k)
    cand = jnp.concatenate(groups, axis=0)
    _, rankc, dc = _topk_distinct(cand, k)
    picked = rankc < k
    z = jnp.sum(jnp.where(picked, jnp.exp(cand - (v1[0] + v2[0])), 0.0), axis=0, keepdims=True)
    count = jnp.zeros(s1.shape, F32)
    row0 = 0
    for i, g in enumerate(groups[:-1]):
        rows = g.shape[0]
        n_i = jnp.sum(jnp.where(picked[row0:row0 + rows], 1.0, 0.0), axis=0, keepdims=True)
        count = count + jnp.where(rank1 == float(i), n_i, 0.0)
        row0 += rows
    tail = jnp.where(picked[row0:], 1.0, 0.0)
    for q in range(tail.shape[0]):
        count = count + jnp.where(rank1 == float(k // 2 + q), tail[q:q + 1], 0.0)
    ok = jnp.logical_and(jnp.logical_and(d1 == k, d2 == k), dc == k)
    distinct = jnp.min(jnp.where(ok, 1.0, 0.0)) > 0.5
    return jnp.exp(s1 - v1[0]) / z, count, jnp.exp(s2 - v2[0]), rank2, distinct


def _peer_topk(q, sub_keys, nheads):
    T = q.shape[0]
    nkeys, half = sub_keys.shape[1], sub_keys.shape[2]
    assert half == LANES and nkeys == LANES
    k = PEER_TOPK
    assert k % (2 * SUBLANES) == 0
    tt = _pick(T, PEER_TOPK_TOKENS)
    out = jax.ShapeDtypeStruct((nheads * nkeys, T), F32)
    out_b = jax.ShapeDtypeStruct((nheads * nkeys, T), BF16)
    out_spec = pl.BlockSpec((nkeys, tt), lambda i, h: (h, i))
    return pl.pallas_call(
        functools.partial(_topk_kernel, k=k),
        grid=(T // tt, nheads),
        in_specs=[pl.BlockSpec((tt, 2 * half), lambda i, h: (i, h)),
                  pl.BlockSpec((2, nkeys, half), lambda i, h: (0, 0, 0))],
        out_specs=[out_spec] * 4,
        out_shape=[out, out, out_b, out_b],
        compiler_params=_params("parallel", "parallel"),
        name="peer_topk",
    )(q, sub_keys)


def _peer_kernel(h_ref, u_ref, vt_ref, kap_ref, cnt_ref, e2_ref, rk2_ref, o_ref, w_sc,
                 *, nheads, nkeys):
    j = pl.program_id(1)
    nj = pl.num_programs(1) - 1
    ka = u_ref.shape[0] // nkeys

    def weights():
        parts = []
        for aa in range(ka):
            a = j * ka + aa
            g = None
            for h in range(nheads):
                kap = kap_ref[pl.ds(h * nkeys + a, 1), :].astype(BF16)
                cnt = cnt_ref[pl.ds(h * nkeys + a, 1), :].astype(BF16)
                e2 = e2_ref[h * nkeys:(h + 1) * nkeys, :]
                rk2 = rk2_ref[h * nkeys:(h + 1) * nkeys, :]
                term = jnp.where(rk2 < cnt, e2, jnp.zeros_like(e2)) * kap
                g = term if g is None else g + term
            parts.append(g)
        gates = jnp.concatenate(parts, axis=0)
        s = _dot_nt(u_ref[...], h_ref[...])
        return _gelu(s).astype(BF16) * gates

    cur = j % 2

    @pl.when(j == 0)
    def _():
        o_ref[...] = jnp.zeros_like(o_ref)
        w_sc[0] = weights()

    @pl.when(jnp.logical_and(j > 0, j < nj))
    def _():
        w_next = weights()
        o_ref[...] += jnp.dot(vt_ref[...], w_sc[1 - cur], preferred_element_type=F32)
        w_sc[cur] = w_next

    @pl.when(j == nj)
    def _():
        o_ref[...] += jnp.dot(vt_ref[...], w_sc[1 - cur], preferred_element_type=F32)


def _peer_tiles(T, E):
    return _pick(T, 512), _pick(E, 512)


def _peer_dense(h2, u_tab, vt_tiles, kap, cnt, e2, rk2, nheads, nkeys):
    T, D = h2.shape
    E = u_tab.shape[0]
    tm, te = _peer_tiles(T, E)
    assert te % nkeys == 0 and vt_tiles.shape == (E // te, D, te)
    once = pl.Buffered(1)
    tab_spec = pl.BlockSpec((nheads * nkeys, tm), lambda i, j: (0, i), pipeline_mode=once)
    nj = E // te
    return pl.pallas_call(
        functools.partial(_peer_kernel, nheads=nheads, nkeys=nkeys),
        grid=(T // tm, nj + 1),
        in_specs=[pl.BlockSpec((tm, D), lambda i, j: (i, 0), pipeline_mode=once),
                  pl.BlockSpec((te, D), lambda i, j: (jnp.minimum(j, nj - 1), 0)),
                  pl.BlockSpec((None, D, te), lambda i, j: (jnp.maximum(j - 1, 0), 0, 0)),
                  tab_spec, tab_spec, tab_spec, tab_spec],
        out_specs=pl.BlockSpec((D, tm), lambda i, j: (0, i)),
        out_shape=jax.ShapeDtypeStruct((D, T), F32),
        scratch_shapes=[pltpu.VMEM((2, te, tm), BF16)],
        compiler_params=_params("parallel", "arbitrary"),
        name="peer_dense",
    )(h2, u_tab, vt_tiles, kap, cnt, e2, rk2)


def _residual_kernel(x_ref, yt_ref, g_ref, o_ref):
    o_ref[0] = x_ref[0] + g_ref[0] * yt_ref[...].T


def _residual(x, y_t, gate):
    B, S, D = x.shape
    ts = _pick(S, 256)
    per_b = S // ts
    blk = pl.BlockSpec((1, ts, D), lambda b, s: (b, s, 0))
    return pl.pallas_call(
        _residual_kernel,
        grid=(B, per_b),
        in_specs=[blk, pl.BlockSpec((D, ts), lambda b, s: (0, b * per_b + s)),
                  pl.BlockSpec((1, 1, D), lambda b, s: (b, 0, 0))],
        out_specs=blk,
        out_shape=jax.ShapeDtypeStruct((B, S, D), F32),
        compiler_params=_params("parallel", "parallel"),
        name="residual",
    )(x, y_t, gate)


def kernel(x, c, positions, norm1_g, norm2_g, w_ada, b_ada, w_in, conv_w, conv_b, w_a, b_a, w_x,
           b_x, lru_lambda, q_norm_g, k_norm_g, lambda_q1, lambda_k1, lambda_q2, lambda_k2,
           subln_g, w_lru_out, w_attn_out, w_o, w_pq, sub_keys, u_tab, v_tab):
    B, S, D = x.shape
    T = B * S
    depth = w_in.shape[0]
    lru_w = conv_w.shape[-1]
    attn_w = w_attn_out.shape[1]
    head_dim = q_norm_g.shape[-1]
    nkeys = sub_keys.shape[2]
    peer_heads = w_pq.shape[2] // (2 * sub_keys.shape[3])
    assert B <= SUBLANES and subln_g.shape[-1] == LANES
    q_col = 2 * lru_w
    k_col = q_col + attn_w
    v_col = k_col + attn_w
    gl_col = v_col + attn_w
    ga_col = gl_col + D

    c_pad = jnp.pad(c, ((0, SUBLANES - B), (0, 0)))
    mod = _ada(c_pad, w_ada, b_ada)[:, :B].reshape(depth, B, N_MOD, 1, D)
    cos_t, sin_t = _rope_tables(positions, head_dim)

    for l in range(depth):
        lam_init = 0.8 - 0.6 * math.exp(-0.3 * l)
        shift1, scale1, gate1 = mod[l, :, 0], mod[l, :, 1], mod[l, :, 2]
        shift2, scale2, gate2 = mod[l, :, 3], mod[l, :, 4], mod[l, :, 5]

        h = _norm_mod(x, norm1_g[l], scale1, shift1)
        z2 = _matmul(h.reshape(T, D), w_in[l].astype(BF16), BF16, "in_proj")
        z = z2.reshape(B, S, -1)
        rec = _lru(z, conv_w[l], conv_b[l], w_a[l], b_a[l], w_x[l], b_x[l], lru_lambda[l], lru_w)
        qn, kn = _qk_prep(z, cos_t, sin_t, q_norm_g[l], k_norm_g[l], q_col, k_col, attn_w, head_dim)
        lams = jnp.stack([lambda_q1[l], lambda_k1[l], lambda_q2[l], lambda_k2[l]])
        att = _attention(qn, kn, z, lams, subln_g[l], v_col, head_dim, lam_init)
        merged = _merge(rec.reshape(T, lru_w), att.reshape(T, attn_w), w_lru_out[l].astype(BF16),
                        w_attn_out[l].astype(BF16), z2, gl_col, ga_col)
        x = _proj_residual(merged, w_o[l].astype(BF16), x.reshape(T, D), gate1, S).reshape(B, S, D)

        h2 = _norm_mod(x, norm2_g[l], scale2, shift2).reshape(T, D)
        pq = _matmul(h2, w_pq[l].astype(BF16), F32, "peer_query")
        kap, cnt, e2, rk2 = _peer_topk(pq, sub_keys[l], peer_heads)
        te = _peer_tiles(T, u_tab.shape[1])[1]
        vt_tiles = v_tab[l].astype(BF16).reshape(-1, te, D).transpose(0, 2, 1)
        y_t = _peer_dense(h2, u_tab[l].astype(BF16), vt_tiles, kap, cnt, e2, rk2,
                          peer_heads, nkeys)
        x = _residual(x, y_t, gate2)
    return x
```
